```python
import jax, jax.numpy as jnp
from jax import lax
import numpy as np

D_MODEL = 2048
BATCH = 8
SEQ = 2048
DEPTH = 2

N_MIXERS = 2
EXPAND = 2
BRANCH_WIDTH = EXPAND * D_MODEL
SC_WIDTH = 3
LRU_CONV_WIDTH = 4
LRU_HEAD_DIM = 256
LRU_HEADS = BRANCH_WIDTH // LRU_HEAD_DIM
RGLRU_C = 8.0
N_CONV_LAYERS = (DEPTH + 1) // N_MIXERS
N_LRU_LAYERS = DEPTH // N_MIXERS
EPS = 1e-6

kernel_name = "hybrid_shortconv_rglru_adaln"


def rmsnorm(x, g):
    xf = x.astype(jnp.float32)
    y = xf * lax.rsqrt(jnp.mean(xf * xf, axis=-1, keepdims=True) + EPS)
    return (y * g.astype(jnp.float32)).astype(x.dtype)


def adaln(h, c, w, b):
    mod = jnp.einsum('bd,df->bf', jax.nn.silu(c), w) + b
    shift, scale, gate = jnp.split(mod, 3, axis=-1)
    h = h * (1.0 + scale[:, None, :]) + shift[:, None, :]
    return h, gate


def causal_depthwise_conv(u, w):
    width, e = w.shape
    rhs = w[:, None, :].astype(u.dtype)
    return lax.conv_general_dilated(
        u, rhs, window_strides=(1,), padding=[(width - 1, 0)],
        dimension_numbers=('NWC', 'WIO', 'NWC'), feature_group_count=e)


def short_conv_mixer(h, w_in, conv_w, w_out):
    proj = jnp.einsum('bsd,de->bse', h, w_in)
    b_gate, c_gate, v, g = jnp.split(proj, 4, axis=-1)
    u = causal_depthwise_conv(c_gate * v, conv_w)
    y = b_gate * u * jax.nn.silu(g)
    return jnp.einsum('bse,ed->bsd', y, w_out)


def _linear_recurrence_combine(left, right):
    a_l, b_l = left
    a_r, b_r = right
    return a_l * a_r, a_r * b_l + b_r


def rglru_mixer(h, w_in, conv_w, conv_b, w_a, b_a, w_x, b_x, lam, w_out):
    proj = jnp.einsum('bsd,de->bse', h, w_in)
    v, g = jnp.split(proj, 2, axis=-1)
    v = causal_depthwise_conv(v, conv_w) + conv_b
    bsz, seq, e = v.shape
    n_heads, head_dim = w_a.shape[0], w_a.shape[1]
    vh = v.reshape(bsz, seq, n_heads, head_dim)
    r = jax.nn.sigmoid(jnp.einsum('bshi,hij->bshj', vh, w_a) + b_a).reshape(bsz, seq, e)
    i = jax.nn.sigmoid(jnp.einsum('bshi,hij->bshj', vh, w_x) + b_x).reshape(bsz, seq, e)
    log_a = -RGLRU_C * r.astype(jnp.float32) * jax.nn.softplus(-lam.astype(jnp.float32))
    a = jnp.exp(log_a)
    norm_mult = jnp.sqrt(-jnp.expm1(2.0 * log_a))
    b = norm_mult * (i * v).astype(jnp.float32)
    _, hs = lax.associative_scan(_linear_recurrence_combine, (a, b), axis=1)
    y = hs.astype(h.dtype) * jax.nn.silu(g)
    return jnp.einsum('bse,ed->bsd', y, w_out)


def setup_inputs(seed: int = 0) -> dict:
    key = jax.random.key(seed)
    ks = jax.random.split(key, 24)
    D, E, H, Dh = D_MODEL, BRANCH_WIDTH, LRU_HEADS, LRU_HEAD_DIM
    nA, nB = N_CONV_LAYERS, N_LRU_LAYERS
    f32 = jnp.float32
    n = lambda k, shape, s: jax.random.normal(k, shape, f32) * s
    x = jax.random.normal(ks[0], (BATCH, SEQ, D), f32)
    c = jax.random.normal(ks[1], (BATCH, D), f32)
    norm_g = 1.0 + n(ks[2], (DEPTH, D), 0.02)
    ada_w = n(ks[3], (DEPTH, D, 3 * D), 0.5 * D ** -0.5)
    ada_b = n(ks[4], (DEPTH, 3 * D), 0.02)
    sc_w_in = n(ks[5], (nA, D, 4 * E), D ** -0.5)
    sc_conv_w = n(ks[6], (nA, SC_WIDTH, E), SC_WIDTH ** -0.5)
    sc_w_out = n(ks[7], (nA, E, D), E ** -0.5)
    lru_w_in = n(ks[8], (nB, D, 2 * E), D ** -0.5)
    lru_conv_w = n(ks[9], (nB, LRU_CONV_WIDTH, E), LRU_CONV_WIDTH ** -0.5)
    lru_conv_b = n(ks[10], (nB, E), 0.02)
    lru_w_a = n(ks[11], (nB, H, Dh, Dh), Dh ** -0.5)
    lru_b_a = n(ks[12], (nB, H, Dh), 0.02)
    lru_w_x = n(ks[13], (nB, H, Dh, Dh), Dh ** -0.5)
    lru_b_x = n(ks[14], (nB, H, Dh), 0.02)
    a_pow_c = jax.random.uniform(ks[15], (nB, E), f32, minval=0.9, maxval=0.999)
    a0 = a_pow_c ** (1.0 / RGLRU_C)
    lru_lambda = jnp.log(a0) - jnp.log1p(-a0)
    lru_w_out = n(ks[16], (nB, E, D), E ** -0.5)
    final_g = 1.0 + n(ks[17], (D,), 0.02)
    return {
        "x": x, "c": c, "norm_g": norm_g, "ada_w": ada_w, "ada_b": ada_b,
        "sc_w_in": sc_w_in, "sc_conv_w": sc_conv_w, "sc_w_out": sc_w_out,
        "lru_w_in": lru_w_in, "lru_conv_w": lru_conv_w, "lru_conv_b": lru_conv_b,
        "lru_w_a": lru_w_a, "lru_b_a": lru_b_a, "lru_w_x": lru_w_x, "lru_b_x": lru_b_x,
        "lru_lambda": lru_lambda, "lru_w_out": lru_w_out, "final_g": final_g,
    }


def reference(x, c, norm_g, ada_w, ada_b, sc_w_in, sc_conv_w, sc_w_out,
              lru_w_in, lru_conv_w, lru_conv_b, lru_w_a, lru_b_a, lru_w_x, lru_b_x,
              lru_lambda, lru_w_out, final_g):
    for layer in range(DEPTH):
        j = layer // N_MIXERS
        h = rmsnorm(x, norm_g[layer])
        h, gate = adaln(h, c, ada_w[layer], ada_b[layer])
        if layer % N_MIXERS == 0:
            y = short_conv_mixer(h, sc_w_in[j], sc_conv_w[j], sc_w_out[j])
        else:
            y = rglru_mixer(h, lru_w_in[j], lru_conv_w[j], lru_conv_b[j],
                            lru_w_a[j], lru_b_a[j], lru_w_x[j], lru_b_x[j],
                            lru_lambda[j], lru_w_out[j])
        x = x + gate[:, None, :] * y
    return rmsnorm(x, final_g)
```

```python
import functools

import jax
import jax.numpy as jnp
from jax import lax
from jax.experimental import pallas as pl
from jax.experimental.pallas import tpu as pltpu

EPS = 1e-6
RGLRU_C = 8.0
LRU_HEAD_DIM = 256

SUBLANES = 8
V7X_VMEM_LIMIT_BYTES = 56 * 1024 * 1024

F32 = jnp.float32
BF16 = jnp.bfloat16


def _sigmoid(z):
    return 1.0 / (1.0 + jnp.exp(-z))


def _silu(z):
    return z * _sigmoid(z)


def _mod_kernel(c_ref, w_ref, b_ref, o_ref):
    s = _silu(c_ref[...]).astype(BF16)
    w = w_ref[...].astype(BF16)
    o_ref[...] = jnp.dot(s, w, preferred_element_type=F32) + b_ref[...]


def _adaln_mod(c, ada_w, ada_b, *, tn):
    depth, d, n = ada_w.shape
    bsz = c.shape[0]
    return pl.pallas_call(
        _mod_kernel,
        grid=(depth, n // tn),
        in_specs=[
            pl.BlockSpec((bsz, d), lambda l, k: (0, 0)),
            pl.BlockSpec((None, d, tn), lambda l, k: (l, 0, k)),
            pl.BlockSpec((None, 1, tn), lambda l, k: (l, 0, k)),
        ],
        out_specs=pl.BlockSpec((None, bsz, tn), lambda l, k: (l, 0, k)),
        out_shape=jax.ShapeDtypeStruct((depth, bsz, n), F32),
        compiler_params=pltpu.CompilerParams(
            dimension_semantics=("arbitrary", "arbitrary"),
            vmem_limit_bytes=V7X_VMEM_LIMIT_BYTES),
        name="adaln_mod",
    )(c, ada_w, ada_b.reshape(depth, 1, n))


def _modulated_norm(x_ref, mod_ref, g_ref, h_ref, *, tm, d):
    g = g_ref[...]
    shift = jnp.tile(mod_ref[:, 0:d], (2, 1))
    scale1 = 1.0 + jnp.tile(mod_ref[:, d:2 * d], (2, 1))
    step = 2 * SUBLANES

    def body(r, carry):
        rows = pl.ds(pl.multiple_of(r * step, step), step)
        x = x_ref[rows, :]
        ms = jnp.mean(x * x, axis=-1, keepdims=True)
        y = x * lax.rsqrt(ms + EPS) * g
        h_ref[rows, :] = (y * scale1 + shift).astype(h_ref.dtype)
        return carry

    lax.fori_loop(0, tm // step, body, 0, unroll=4)


def _zero_rows(o_ref, *, tm):
    def body(r, carry):
        rows = pl.ds(pl.multiple_of(r * SUBLANES, SUBLANES), SUBLANES)
        o_ref[rows, :] = jnp.zeros((SUBLANES, o_ref.shape[1]), o_ref.dtype)
        return carry

    lax.fori_loop(0, tm // SUBLANES, body, 0, unroll=8)


def _sconv_kernel(x_ref, mod_ref, g_ref, wb_ref, wc_ref, wv_ref, wg_ref, cw_ref, wo_ref,
                  o_ref, h_ref, cv_ref, carry_ref, *, tm, te, d, taps):
    i = pl.program_id(0)
    j = pl.program_id(1)
    nj = pl.num_programs(1)
    halo = (taps - 1) * SUBLANES

    @pl.when(j == 0)
    def _():
        _modulated_norm(x_ref, mod_ref, g_ref, h_ref, tm=tm, d=d)
        _zero_rows(o_ref, tm=tm)

    @pl.when(i == 0)
    def _():
        cv_ref[0:halo, :] = jnp.zeros((halo, te), F32)

    @pl.when(i > 0)
    def _():
        cv_ref[0:halo, :] = carry_ref[j]

    h = h_ref[...]
    pc = jnp.dot(h, wc_ref[...], preferred_element_type=F32)
    pv = jnp.dot(h, wv_ref[...], preferred_element_type=F32)
    cv_ref[halo:halo + tm, :] = pc * pv
    carry_ref[j] = cv_ref[tm:tm + halo, :]

    u = cw_ref[0:1, :] * cv_ref[0:tm, :]
    for k in range(1, taps):
        u = u + cw_ref[k:k + 1, :] * cv_ref[k * SUBLANES:k * SUBLANES + tm, :]
    pb = jnp.dot(h, wb_ref[...], preferred_element_type=F32)
    pg = jnp.dot(h, wg_ref[...], preferred_element_type=F32)
    y = (pb * u * _silu(pg)).astype(BF16)
    o_ref[...] += jnp.dot(y, wo_ref[...], preferred_element_type=F32)

    @pl.when(j == nj - 1)
    def _():
        gate = mod_ref[:, 2 * d:3 * d]

        def body(r, carry):
            rows = pl.ds(pl.multiple_of(r * SUBLANES, SUBLANES), SUBLANES)
            o_ref[rows, :] = x_ref[rows, :] + gate * o_ref[rows, :]
            return carry

        lax.fori_loop(0, tm // SUBLANES, body, 0, unroll=8)


def _sconv_layer(xt, mod, g, w_in, conv_w, w_out, *, tm, te):
    t, d = xt.shape
    e = w_out.shape[0]
    taps = conv_w.shape[0]
    nj = e // te
    halo = (taps - 1) * SUBLANES
    col = lambda c: (lambda i, j: (0, c * nj + j))
    return pl.pallas_call(
        functools.partial(_sconv_kernel, tm=tm, te=te, d=d, taps=taps),
        grid=(t // tm, nj),
        in_specs=[
            pl.BlockSpec((tm, d), lambda i, j: (i, 0)),
            pl.BlockSpec(mod.shape, lambda i, j: (0, 0)),
            pl.BlockSpec((1, d), lambda i, j: (0, 0)),
            pl.BlockSpec((d, te), col(0)),
            pl.BlockSpec((d, te), col(1)),
            pl.BlockSpec((d, te), col(2)),
            pl.BlockSpec((d, te), col(3)),
            pl.BlockSpec((taps, te), lambda i, j: (0, j)),
            pl.BlockSpec((te, d), lambda i, j: (j, 0)),
        ],
        out_specs=pl.BlockSpec((tm, d), lambda i, j: (i, 0)),
        out_shape=jax.ShapeDtypeStruct((t, d), F32),
        scratch_shapes=[
            pltpu.VMEM((tm, d), BF16),
            pltpu.VMEM((tm + halo, te), F32),
            pltpu.VMEM((nj, halo, te), F32),
        ],
        compiler_params=pltpu.CompilerParams(
            dimension_semantics=("arbitrary", "arbitrary"),
            vmem_limit_bytes=V7X_VMEM_LIMIT_BYTES),
        name="sconv_layer",
    )(xt, mod, g, w_in, w_in, w_in, w_in, conv_w, w_out)


def _rglru_kernel(x_ref, mod_ref, g_ref, wv_ref, wg_ref, cw_ref, cb_ref, wa_ref, ba_ref,
                  wx_ref, bx_ref, lam_ref, wo_ref, fg_ref,
                  o_ref, h_ref, v_ref, a_ref, b_ref, vcarry_ref, hcarry_ref,
                  *, tm, te, d, taps):
    i = pl.program_id(0)
    j = pl.program_id(1)
    nj = pl.num_programs(1)
    halo = (taps - 1) * SUBLANES

    @pl.when(j == 0)
    def _():
        _modulated_norm(x_ref, mod_ref, g_ref, h_ref, tm=tm, d=d)
        _zero_rows(o_ref, tm=tm)

    @pl.when(i == 0)
    def _():
        v_ref[0:halo, :] = jnp.zeros((halo, te), F32)
        hcarry_ref[j] = jnp.zeros((SUBLANES, te), F32)

    @pl.when(i > 0)
    def _():
        v_ref[0:halo, :] = vcarry_ref[j]

    h = h_ref[...]
    v_ref[halo:halo + tm, :] = jnp.dot(h, wv_ref[...], preferred_element_type=F32)
    vcarry_ref[j] = v_ref[tm:tm + halo, :]
    vc = cb_ref[...] + cw_ref[0:1, :] * v_ref[0:tm, :]
    for k in range(1, taps):
        vc = vc + cw_ref[k:k + 1, :] * v_ref[k * SUBLANES:k * SUBLANES + tm, :]

    vcb = vc.astype(BF16)
    za, zx = [], []
    for hh in range(te // LRU_HEAD_DIM):
        cols = slice(hh * LRU_HEAD_DIM, (hh + 1) * LRU_HEAD_DIM)
        za.append(jnp.dot(vcb[:, cols], wa_ref[hh], preferred_element_type=F32))
        zx.append(jnp.dot(vcb[:, cols], wx_ref[hh], preferred_element_type=F32))
    r = _sigmoid(jnp.concatenate(za, axis=-1) + ba_ref[...])
    ig = _sigmoid(jnp.concatenate(zx, axis=-1) + bx_ref[...])

    z = -lam_ref[...]
    softplus = jnp.maximum(z, 0.0) + jnp.log1p(jnp.exp(-jnp.abs(z)))
    log_a = r * (-RGLRU_C * softplus)
    a = jnp.exp(log_a)
    norm = jnp.sqrt(-jnp.tanh(log_a) * (a * a + 1.0))
    a_ref[...] = a
    b_ref[...] = norm * (ig * vc)

    hprev = hcarry_ref[j]
    for t in range(tm // SUBLANES):
        rows = slice(t * SUBLANES, (t + 1) * SUBLANES)
        hprev = a_ref[rows, :] * hprev + b_ref[rows, :]
        b_ref[rows, :] = hprev
    hcarry_ref[j] = hprev

    pg = jnp.dot(h, wg_ref[...], preferred_element_type=F32)
    y = (b_ref[...] * _silu(pg)).astype(BF16)
    o_ref[...] += jnp.dot(y, wo_ref[...], preferred_element_type=F32)

    @pl.when(j == nj - 1)
    def _():
        gate = mod_ref[:, 2 * d:3 * d]
        fg = fg_ref[...]

        def body(r_, carry):
            rows = pl.ds(pl.multiple_of(r_ * SUBLANES, SUBLANES), SUBLANES)
            xn = x_ref[rows, :] + gate * o_ref[rows, :]
            ms = jnp.mean(xn * xn, axis=-1, keepdims=True)
            o_ref[rows, :] = xn * lax.rsqrt(ms + EPS) * fg
            return carry

        lax.fori_loop(0, tm // SUBLANES, body, 0, unroll=8)


def _rglru_layer(xt, mod, g, w_in, conv_w, conv_b, w_a, b_a, w_x, b_x, lam, w_out,
                 final_g, *, tm, te):
    t, d = xt.shape
    e = w_out.shape[0]
    taps = conv_w.shape[0]
    nj = e // te
    hps = te // LRU_HEAD_DIM
    halo = (taps - 1) * SUBLANES
    row = lambda a: a.reshape(1, e)
    return pl.pallas_call(
        functools.partial(_rglru_kernel, tm=tm, te=te, d=d, taps=taps),
        grid=(t // tm, nj),
        in_specs=[
            pl.BlockSpec((tm, d), lambda i, j: (i, 0)),
            pl.BlockSpec(mod.shape, lambda i, j: (0, 0)),
            pl.BlockSpec((1, d), lambda i, j: (0, 0)),
            pl.BlockSpec((d, te), lambda i, j: (0, j)),
            pl.BlockSpec((d, te), lambda i, j: (0, nj + j)),
            pl.BlockSpec((taps, te), lambda i, j: (0, j)),
            pl.BlockSpec((1, te), lambda i, j: (0, j)),
            pl.BlockSpec((hps, LRU_HEAD_DIM, LRU_HEAD_DIM), lambda i, j: (j, 0, 0)),
            pl.BlockSpec((1, te), lambda i, j: (0, j)),
            pl.BlockSpec((hps, LRU_HEAD_DIM, LRU_HEAD_DIM), lambda i, j: (j, 0, 0)),
            pl.BlockSpec((1, te), lambda i, j: (0, j)),
            pl.BlockSpec((1, te), lambda i, j: (0, j)),
            pl.BlockSpec((te, d), lambda i, j: (j, 0)),
            pl.BlockSpec((1, d), lambda i, j: (0, 0)),
        ],
        out_specs=pl.BlockSpec((tm, d), lambda i, j: (i, 0)),
        out_shape=jax.ShapeDtypeStruct((t, d), F32),
        scratch_shapes=[
            pltpu.VMEM((tm, d), BF16),
            pltpu.VMEM((tm + halo, te), F32),
            pltpu.VMEM((tm, te), F32),
            pltpu.VMEM((tm, te), F32),
            pltpu.VMEM((nj, halo, te), F32),
            pltpu.VMEM((nj, SUBLANES, te), F32),
        ],
        compiler_params=pltpu.CompilerParams(
            dimension_semantics=("arbitrary", "arbitrary"),
            vmem_limit_bytes=V7X_VMEM_LIMIT_BYTES),
        name="rglru_layer",
    )(xt, mod, g, w_in, w_in, conv_w, row(conv_b), w_a, row(b_a), w_x, row(b_x), row(lam),
      w_out, final_g.reshape(1, d))


def kernel(x, c, norm_g, ada_w, ada_b, sc_w_in, sc_conv_w, sc_w_out, lru_w_in, lru_conv_w,
           lru_conv_b, lru_w_a, lru_b_a, lru_w_x, lru_b_x, lru_lambda, lru_w_out, final_g):
    bsz, seq, d = x.shape
    assert bsz == SUBLANES, "rows are grouped so that one f32 vreg holds all sequences"
    assert ada_w.shape[0] == 2 and sc_w_in.shape[0] == 1 and lru_w_in.shape[0] == 1

    mod = _adaln_mod(c, ada_w, ada_b, tn=768)
    xt = jnp.transpose(x, (1, 0, 2)).reshape(seq * bsz, d)

    x1 = _sconv_layer(xt, mod[0], norm_g[0:1], sc_w_in[0].astype(BF16), sc_conv_w[0],
                      sc_w_out[0].astype(BF16), tm=512, te=256)
    out = _rglru_layer(x1, mod[1], norm_g[1:2], lru_w_in[0].astype(BF16), lru_conv_w[0],
                       lru_conv_b[0], lru_w_a[0].astype(BF16), lru_b_a[0],
                       lru_w_x[0].astype(BF16), lru_b_x[0], lru_lambda[0],
                       lru_w_out[0].astype(BF16), final_g, tm=512, te=512)
    return jnp.transpose(out.reshape(seq, bsz, d), (1, 0, 2))
```

```python
import functools

import jax
import jax.numpy as jnp
from jax import lax
from jax.experimental import pallas as pl
from jax.experimental.pallas import tpu as pltpu

EPS = 1e-6
RGLRU_C = 8.0
LRU_HEAD_DIM = 256

SUBLANES = 8
V7X_VMEM_LIMIT_BYTES = 56 * 1024 * 1024

F32 = jnp.float32
BF16 = jnp.bfloat16


NORM_GROUP = 8


def _sigmoid(z):
    return 0.5 * jnp.tanh(0.5 * z) + 0.5


def _silu(z):
    hz = 0.5 * z
    return hz * jnp.tanh(hz) + hz


def _sqrt_nonneg(x):
    return jnp.where(x > 0.0, x * lax.rsqrt(x), 0.0)


def _mod_kernel(c_ref, w_ref, b_ref, o_ref):
    s = _silu(c_ref[...]).astype(BF16)
    w = w_ref[...].astype(BF16)
    o_ref[...] = jnp.dot(s, w, preferred_element_type=F32) + b_ref[...]


def _adaln_mod(c, ada_w, ada_b, *, tn):
    depth, d, n = ada_w.shape
    bsz = c.shape[0]
    return pl.pallas_call(
        _mod_kernel,
        grid=(depth, n // tn),
        in_specs=[
            pl.BlockSpec((bsz, d), lambda l, k: (0, 0)),
            pl.BlockSpec((None, d, tn), lambda l, k: (l, 0, k)),
            pl.BlockSpec((None, 1, tn), lambda l, k: (l, 0, k)),
        ],
        out_specs=pl.BlockSpec((None, bsz, tn), lambda l, k: (l, 0, k)),
        out_shape=jax.ShapeDtypeStruct((depth, bsz, n), F32),
        compiler_params=pltpu.CompilerParams(
            dimension_semantics=("arbitrary", "arbitrary"),
            vmem_limit_bytes=V7X_VMEM_LIMIT_BYTES),
        name="adaln_mod",
    )(c, ada_w, ada_b.reshape(depth, 1, n))


def _modulated_norm(x_ref, mod_ref, g_ref, h_ref, *, tm, d):
    g = g_ref[...]
    shift = jnp.tile(mod_ref[:, 0:d], (2, 1))
    scale1 = 1.0 + jnp.tile(mod_ref[:, d:2 * d], (2, 1))
    step = 2 * SUBLANES

    def body(r, carry):
        base = r * (NORM_GROUP * step)
        rows = [pl.ds(pl.multiple_of(base + k * step, step), step) for k in range(NORM_GROUP)]
        inv = []
        for k in range(NORM_GROUP):
            x = x_ref[rows[k], :]
            inv.append(lax.rsqrt(jnp.mean(x * x, axis=-1, keepdims=True) + EPS))
        for k in range(NORM_GROUP):
            y = x_ref[rows[k], :] * inv[k] * g
            h_ref[rows[k], :] = (y * scale1 + shift).astype(h_ref.dtype)
        return carry

    lax.fori_loop(0, tm // (NORM_GROUP * step), body, 0)


def _zero_rows(o_ref, *, tm):
    def body(r, carry):
        rows = pl.ds(pl.multiple_of(r * SUBLANES, SUBLANES), SUBLANES)
        o_ref[rows, :] = jnp.zeros((SUBLANES, o_ref.shape[1]), o_ref.dtype)
        return carry

    lax.fori_loop(0, tm // SUBLANES, body, 0, unroll=8)


def _sconv_kernel(x_ref, mod_ref, g_ref, wb_ref, wc_ref, wv_ref, wg_ref, cw_ref, wo_ref,
                  o_ref, h_ref, cv_ref, carry_ref, *, tm, te, d, taps):
    i = pl.program_id(0)
    j = pl.program_id(1)
    nj = pl.num_programs(1)
    halo = (taps - 1) * SUBLANES

    @pl.when(j == 0)
    def _():
        _modulated_norm(x_ref, mod_ref, g_ref, h_ref, tm=tm, d=d)
        _zero_rows(o_ref, tm=tm)

    @pl.when(i == 0)
    def _():
        cv_ref[0:halo, :] = jnp.zeros((halo, te), F32)

    @pl.when(i > 0)
    def _():
        cv_ref[0:halo, :] = carry_ref[j]

    h = h_ref[...]
    pc = jnp.dot(h, wc_ref[...], preferred_element_type=F32)
    pv = jnp.dot(h, wv_ref[...], preferred_element_type=F32)
    cv_ref[halo:halo + tm, :] = pc * pv
    carry_ref[j] = cv_ref[tm:tm + halo, :]

    u = cw_ref[0:1, :] * cv_ref[0:tm, :]
    for k in range(1, taps):
        u = u + cw_ref[k:k + 1, :] * cv_ref[k * SUBLANES:k * SUBLANES + tm, :]
    pb = jnp.dot(h, wb_ref[...], preferred_element_type=F32)
    pg = jnp.dot(h, wg_ref[...], preferred_element_type=F32)
    y = (pb * u * _silu(pg)).astype(BF16)
    o_ref[...] += jnp.dot(y, wo_ref[...], preferred_element_type=F32)

    @pl.when(j == nj - 1)
    def _():
        gate = mod_ref[:, 2 * d:3 * d]

        def body(r, carry):
            rows = pl.ds(pl.multiple_of(r * SUBLANES, SUBLANES), SUBLANES)
            o_ref[rows, :] = x_ref[rows, :] + gate * o_ref[rows, :]
            return carry

        lax.fori_loop(0, tm // SUBLANES, body, 0, unroll=8)


def _sconv_layer(xt, mod, g, w_in, conv_w, w_out, *, tm, te):
    t, d = xt.shape
    e = w_out.shape[0]
    taps = conv_w.shape[0]
    nj = e // te
    halo = (taps - 1) * SUBLANES
    col = lambda c: (lambda i, j: (0, c * nj + j))
    return pl.pallas_call(
        functools.partial(_sconv_kernel, tm=tm, te=te, d=d, taps=taps),
        grid=(t // tm, nj),
        in_specs=[
            pl.BlockSpec((tm, d), lambda i, j: (i, 0)),
            pl.BlockSpec(mod.shape, lambda i, j: (0, 0)),
            pl.BlockSpec((1, d), lambda i, j: (0, 0)),
            pl.BlockSpec((d, te), col(0)),
            pl.BlockSpec((d, te), col(1)),
            pl.BlockSpec((d, te), col(2)),
            pl.BlockSpec((d, te), col(3)),
            pl.BlockSpec((taps, te), lambda i, j: (0, j)),
            pl.BlockSpec((te, d), lambda i, j: (j, 0)),
        ],
        out_specs=pl.BlockSpec((tm, d), lambda i, j: (i, 0)),
        out_shape=jax.ShapeDtypeStruct((t, d), F32),
        scratch_shapes=[
            pltpu.VMEM((tm, d), BF16),
            pltpu.VMEM((tm + halo, te), F32),
            pltpu.VMEM((nj, halo, te), F32),
        ],
        compiler_params=pltpu.CompilerParams(
            dimension_semantics=("arbitrary", "arbitrary"),
            vmem_limit_bytes=V7X_VMEM_LIMIT_BYTES),
        name="sconv_layer",
    )(xt, mod, g, w_in, w_in, w_in, w_in, conv_w, w_out)


def _rglru_kernel(x_ref, mod_ref, g_ref, wv_ref, wg_ref, cw_ref, cb_ref, wa_ref, ba_ref,
                  wx_ref, bx_ref, lam_ref, wo_ref, fg_ref,
                  o_ref, h_ref, v_ref, a_ref, b_ref, vcarry_ref, hcarry_ref,
                  *, tm, te, d, taps):
    i = pl.program_id(0)
    j = pl.program_id(1)
    nj = pl.num_programs(1)
    halo = (taps - 1) * SUBLANES

    @pl.when(j == 0)
    def _():
        _modulated_norm(x_ref, mod_ref, g_ref, h_ref, tm=tm, d=d)
        _zero_rows(o_ref, tm=tm)

    @pl.when(i == 0)
    def _():
        v_ref[0:halo, :] = jnp.zeros((halo, te), F32)
        hcarry_ref[j] = jnp.zeros((SUBLANES, te), F32)

    @pl.when(i > 0)
    def _():
        v_ref[0:halo, :] = vcarry_ref[j]

    h = h_ref[...]
    v_ref[halo:halo + tm, :] = jnp.dot(h, wv_ref[...], preferred_element_type=F32)
    vcarry_ref[j] = v_ref[tm:tm + halo, :]
    vc = cb_ref[...] + cw_ref[0:1, :] * v_ref[0:tm, :]
    for k in range(1, taps):
        vc = vc + cw_ref[k:k + 1, :] * v_ref[k * SUBLANES:k * SUBLANES + tm, :]

    vcb = vc.astype(BF16)
    za, zx = [], []
    for hh in range(te // LRU_HEAD_DIM):
        cols = slice(hh * LRU_HEAD_DIM, (hh + 1) * LRU_HEAD_DIM)
        za.append(jnp.dot(vcb[:, cols], wa_ref[hh], preferred_element_type=F32))
        zx.append(jnp.dot(vcb[:, cols], wx_ref[hh], preferred_element_type=F32))
    r = _sigmoid(jnp.concatenate(za, axis=-1) + ba_ref[...])
    ig = _sigmoid(jnp.concatenate(zx, axis=-1) + bx_ref[...])

    z = -lam_ref[...]
    softplus = jnp.maximum(z, 0.0) + jnp.log1p(jnp.exp(-jnp.abs(z)))
    log_a = r * (-RGLRU_C * softplus)
    a = jnp.exp(log_a)
    norm = _sqrt_nonneg(-jnp.tanh(log_a) * (a * a + 1.0))
    a_ref[...] = a
    b_ref[...] = norm * (ig * vc)

    hprev = hcarry_ref[j]
    for t in range(tm // SUBLANES):
        rows = slice(t * SUBLANES, (t + 1) * SUBLANES)
        hprev = a_ref[rows, :] * hprev + b_ref[rows, :]
        b_ref[rows, :] = hprev
    hcarry_ref[j] = hprev

    pg = jnp.dot(h, wg_ref[...], preferred_element_type=F32)
    y = (b_ref[...] * _silu(pg)).astype(BF16)
    o_ref[...] += jnp.dot(y, wo_ref[...], preferred_element_type=F32)

    @pl.when(j == nj - 1)
    def _():
        gate = mod_ref[:, 2 * d:3 * d]
        fg = fg_ref[...]

        def body(r_, carry):
            base = r_ * (2 * NORM_GROUP * SUBLANES)
            rows = [pl.ds(pl.multiple_of(base + k * SUBLANES, SUBLANES), SUBLANES)
                    for k in range(2 * NORM_GROUP)]
            inv = []
            for k in range(2 * NORM_GROUP):
                xn = x_ref[rows[k], :] + gate * o_ref[rows[k], :]
                o_ref[rows[k], :] = xn
                inv.append(lax.rsqrt(jnp.mean(xn * xn, axis=-1, keepdims=True) + EPS))
            for k in range(2 * NORM_GROUP):
                o_ref[rows[k], :] = o_ref[rows[k], :] * inv[k] * fg
            return carry

        lax.fori_loop(0, tm // (2 * NORM_GROUP * SUBLANES), body, 0)


def _rglru_layer(xt, mod, g, w_in, conv_w, conv_b, w_a, b_a, w_x, b_x, lam, w_out,
                 final_g, *, tm, te):
    t, d = xt.shape
    e = w_out.shape[0]
    taps = conv_w.shape[0]
    nj = e // te
    hps = te // LRU_HEAD_DIM
    halo = (taps - 1) * SUBLANES
    row = lambda a: a.reshape(1, e)
    return pl.pallas_call(
        functools.partial(_rglru_kernel, tm=tm, te=te, d=d, taps=taps),
        grid=(t // tm, nj),
        in_specs=[
            pl.BlockSpec((tm, d), lambda i, j: (i, 0)),
            pl.BlockSpec(mod.shape, lambda i, j: (0, 0)),
            pl.BlockSpec((1, d), lambda i, j: (0, 0)),
            pl.BlockSpec((d, te), lambda i, j: (0, j)),
            pl.BlockSpec((d, te), lambda i, j: (0, nj + j)),
            pl.BlockSpec((taps, te), lambda i, j: (0, j)),
            pl.BlockSpec((1, te), lambda i, j: (0, j)),
            pl.BlockSpec((hps, LRU_HEAD_DIM, LRU_HEAD_DIM), lambda i, j: (j, 0, 0)),
            pl.BlockSpec((1, te), lambda i, j: (0, j)),
            pl.BlockSpec((hps, LRU_HEAD_DIM, LRU_HEAD_DIM), lambda i, j: (j, 0, 0)),
            pl.BlockSpec((1, te), lambda i, j: (0, j)),
            pl.BlockSpec((1, te), lambda i, j: (0, j)),
            pl.BlockSpec((te, d), lambda i, j: (j, 0)),
            pl.BlockSpec((1, d), lambda i, j: (0, 0)),
        ],
        out_specs=pl.BlockSpec((tm, d), lambda i, j: (i, 0)),
        out_shape=jax.ShapeDtypeStruct((t, d), F32),
        scratch_shapes=[
            pltpu.VMEM((tm, d), BF16),
            pltpu.VMEM((tm + halo, te), F32),
            pltpu.VMEM((tm, te), F32),
            pltpu.VMEM((tm, te), F32),
            pltpu.VMEM((nj, halo, te), F32),
            pltpu.VMEM((nj, SUBLANES, te), F32),
        ],
        compiler_params=pltpu.CompilerParams(
            dimension_semantics=("arbitrary", "arbitrary"),
            vmem_limit_bytes=V7X_VMEM_LIMIT_BYTES),
        name="rglru_layer",
    )(xt, mod, g, w_in, w_in, conv_w, row(conv_b), w_a, row(b_a), w_x, row(b_x), row(lam),
      w_out, final_g.reshape(1, d))


def kernel(x, c, norm_g, ada_w, ada_b, sc_w_in, sc_conv_w, sc_w_out, lru_w_in, lru_conv_w,
           lru_conv_b, lru_w_a, lru_b_a, lru_w_x, lru_b_x, lru_lambda, lru_w_out, final_g):
    bsz, seq, d = x.shape
    assert bsz == SUBLANES, "rows are grouped so that one f32 vreg holds all sequences"
    assert ada_w.shape[0] == 2 and sc_w_in.shape[0] == 1 and lru_w_in.shape[0] == 1

    mod = _adaln_mod(c, ada_w, ada_b, tn=768)
    xt = jnp.transpose(x, (1, 0, 2)).reshape(seq * bsz, d)

    x1 = _sconv_layer(xt, mod[0], norm_g[0:1], sc_w_in[0].astype(BF16), sc_conv_w[0],
                      sc_w_out[0].astype(BF16), tm=512, te=256)
    out = _rglru_layer(x1, mod[1], norm_g[1:2], lru_w_in[0].astype(BF16), lru_conv_w[0],
                       lru_conv_b[0], lru_w_a[0].astype(BF16), lru_b_a[0],
                       lru_w_x[0].astype(BF16), lru_b_x[0], lru_lambda[0],
                       lru_w_out[0].astype(BF16), final_g, tm=512, te=512)
    return jnp.transpose(out.reshape(seq, bsz, d), (1, 0, 2))
```

```python
import functools

import jax
import jax.numpy as jnp
from jax import lax
from jax.experimental import pallas as pl
from jax.experimental.pallas import tpu as pltpu

EPS = 1e-6
RGLRU_C = 8.0
LRU_HEAD_DIM = 256

SUBLANES = 8
PACKED_ROWS = 2 * SUBLANES
NORM_GROUP = 8
V7X_VMEM_LIMIT_BYTES = 56 * 1024 * 1024

F32 = jnp.float32
BF16 = jnp.bfloat16


def _sigmoid(z):
    return 0.5 * jnp.tanh(0.5 * z) + 0.5


def _silu(z):
    hz = 0.5 * z
    return hz * jnp.tanh(hz) + hz


def _sqrt_nonneg(x):
    return jnp.where(x > 0.0, x * lax.rsqrt(x), 0.0)


def _inv_rms(x):
    return lax.rsqrt(jnp.mean(x * x, axis=-1, keepdims=True) + EPS)


def _dot(a, b):
    return jnp.dot(a, b, preferred_element_type=F32)


def _mod_kernel(c_ref, w_ref, b_ref, o_ref):
    s = _silu(c_ref[...]).astype(BF16)
    w = w_ref[...].astype(BF16)
    o_ref[...] = _dot(s, w) + b_ref[...]


def _adaln_mod(c, ada_w, ada_b, *, tn):
    depth, d, n = ada_w.shape
    bsz = c.shape[0]
    return pl.pallas_call(
        _mod_kernel,
        grid=(depth, n // tn),
        in_specs=[
            pl.BlockSpec((bsz, d), lambda l, k: (0, 0)),
            pl.BlockSpec((None, d, tn), lambda l, k: (l, 0, k)),
            pl.BlockSpec((None, 1, tn), lambda l, k: (l, 0, k)),
        ],
        out_specs=pl.BlockSpec((None, bsz, tn), lambda l, k: (l, 0, k)),
        out_shape=jax.ShapeDtypeStruct((depth, bsz, n), F32),
        compiler_params=pltpu.CompilerParams(
            dimension_semantics=("arbitrary", "arbitrary"),
            vmem_limit_bytes=V7X_VMEM_LIMIT_BYTES),
        name="adaln_mod",
    )(c, ada_w, ada_b.reshape(depth, 1, n))


def _modulate_rows(load_block, xt_ref, mod_ref, g_ref, h_ref, *, tm, d):
    g = g_ref[...]
    shift = jnp.tile(mod_ref[:, 0:d], (2, 1))
    scale1 = 1.0 + jnp.tile(mod_ref[:, d:2 * d], (2, 1))

    def body(r, carry):
        blocks = [r * NORM_GROUP + k for k in range(NORM_GROUP)]
        rows = [pl.ds(pl.multiple_of(b * PACKED_ROWS, PACKED_ROWS), PACKED_ROWS)
                for b in blocks]
        inv = []
        for k in range(NORM_GROUP):
            x = load_block(blocks[k])
            if xt_ref is not None:
                xt_ref[rows[k], :] = x
            inv.append(_inv_rms(x))
        for k in range(NORM_GROUP):
            x = load_block(blocks[k]) if xt_ref is None else xt_ref[rows[k], :]
            h_ref[rows[k], :] = (x * inv[k] * g * scale1 + shift).astype(h_ref.dtype)
        return carry

    lax.fori_loop(0, tm // (NORM_GROUP * PACKED_ROWS), body, 0)


def _zero_rows(ref, *, tm):
    def body(r, carry):
        rows = pl.ds(pl.multiple_of(r * SUBLANES, SUBLANES), SUBLANES)
        ref[rows, :] = jnp.zeros((SUBLANES, ref.shape[1]), ref.dtype)
        return carry

    lax.fori_loop(0, tm // SUBLANES, body, 0, unroll=8)


def _causal_taps(ext_ref, cw_ref, *, tm, taps):
    acc = cw_ref[0:1, :] * ext_ref[0:tm, :]
    for k in range(1, taps):
        acc = acc + cw_ref[k:k + 1, :] * ext_ref[k * SUBLANES:k * SUBLANES + tm, :]
    return acc


def _sconv_kernel(x_ref, mod_ref, g_ref, wb_ref, wc_ref, wv_ref, wg_ref, cw_ref, wo_ref,
                  o_ref, xt_ref, h_ref, cv_ref, carry_ref, *, tm, te, d, taps, nj):
    i = pl.program_id(0)
    j = pl.program_id(1)
    halo = (taps - 1) * SUBLANES

    @pl.when(j == 0)
    def _():
        @pl.when(i == 0)
        def _():
            carry_ref[...] = jnp.zeros(carry_ref.shape, F32)

        def load_block(k16):
            t0 = k16 * 2
            return jnp.concatenate([x_ref[:, t0, :], x_ref[:, t0 + 1, :]], axis=0)

        _modulate_rows(load_block, xt_ref, mod_ref, g_ref, h_ref, tm=tm, d=d)
        _zero_rows(o_ref, tm=tm)

    h = h_ref[...]
    cv_ref[0:halo, :] = carry_ref[j]
    cv_ref[halo:halo + tm, :] = _dot(h, wc_ref[...]) * _dot(h, wv_ref[...])
    carry_ref[j] = cv_ref[tm:tm + halo, :]
    u = _causal_taps(cv_ref, cw_ref, tm=tm, taps=taps)
    y = (_dot(h, wb_ref[...]) * u * _silu(_dot(h, wg_ref[...]))).astype(BF16)
    o_ref[...] += _dot(y, wo_ref[...])

    @pl.when(j == nj - 1)
    def _():
        gate = mod_ref[:, 2 * d:3 * d]

        def body(r, carry):
            rows = pl.ds(pl.multiple_of(r * SUBLANES, SUBLANES), SUBLANES)
            o_ref[rows, :] = xt_ref[rows, :] + gate * o_ref[rows, :]
            return carry

        lax.fori_loop(0, tm // SUBLANES, body, 0, unroll=8)


def _sconv_layer(x, mod, g, w_in, conv_w, w_out, *, tm, te):
    bsz, seq, d = x.shape
    e = w_out.shape[0]
    taps = conv_w.shape[0]
    nj = e // te
    ts = tm // bsz
    halo = (taps - 1) * SUBLANES
    col = lambda c: (lambda i, j: (0, c * nj + j))
    return pl.pallas_call(
        functools.partial(_sconv_kernel, tm=tm, te=te, d=d, taps=taps, nj=nj),
        grid=(seq // ts, nj),
        in_specs=[
            pl.BlockSpec((bsz, ts, d), lambda i, j: (0, i, 0)),
            pl.BlockSpec(mod.shape, lambda i, j: (0, 0)),
            pl.BlockSpec((1, d), lambda i, j: (0, 0)),
            pl.BlockSpec((d, te), col(0)),
            pl.BlockSpec((d, te), col(1)),
            pl.BlockSpec((d, te), col(2)),
            pl.BlockSpec((d, te), col(3)),
            pl.BlockSpec((taps, te), lambda i, j: (0, j)),
            pl.BlockSpec((te, d), lambda i, j: (j, 0)),
        ],
        out_specs=pl.BlockSpec((tm, d), lambda i, j: (i, 0)),
        out_shape=jax.ShapeDtypeStruct((seq * bsz, d), F32),
        scratch_shapes=[
            pltpu.VMEM((tm, d), F32),
            pltpu.VMEM((tm, d), BF16),
            pltpu.VMEM((tm + halo, te), F32),
            pltpu.VMEM((nj, halo, te), F32),
        ],
        compiler_params=pltpu.CompilerParams(
            dimension_semantics=("arbitrary", "arbitrary"),
            vmem_limit_bytes=V7X_VMEM_LIMIT_BYTES),
        name="sconv_layer",
    )(x, mod, g, w_in, w_in, w_in, w_in, conv_w, w_out)


def _rglru_kernel(x_ref, mod_ref, g_ref, wv_ref, wg_ref, cw_ref, cb_ref, wa_ref, ba_ref,
                  wx_ref, bx_ref, lam_ref, wo_ref, fg_ref,
                  o_ref, acc_ref, h_ref, v_ref, a_ref, b_ref, vcarry_ref, hcarry_ref,
                  *, tm, te, d, taps, nj):
    i = pl.program_id(0)
    j = pl.program_id(1)
    halo = (taps - 1) * SUBLANES

    @pl.when(j == 0)
    def _():
        @pl.when(i == 0)
        def _():
            vcarry_ref[...] = jnp.zeros(vcarry_ref.shape, F32)
            hcarry_ref[...] = jnp.zeros(hcarry_ref.shape, F32)

        def load_block(k16):
            return x_ref[pl.ds(pl.multiple_of(k16 * PACKED_ROWS, PACKED_ROWS), PACKED_ROWS), :]

        _modulate_rows(load_block, None, mod_ref, g_ref, h_ref, tm=tm, d=d)
        _zero_rows(acc_ref, tm=tm)

    h = h_ref[...]
    v_ref[0:halo, :] = vcarry_ref[j]
    v_ref[halo:halo + tm, :] = _dot(h, wv_ref[...])
    vcarry_ref[j] = v_ref[tm:tm + halo, :]
    vc = cb_ref[...] + _causal_taps(v_ref, cw_ref, tm=tm, taps=taps)

    vcb = vc.astype(BF16)
    za, zx = [], []
    for hh in range(te // LRU_HEAD_DIM):
        cols = slice(hh * LRU_HEAD_DIM, (hh + 1) * LRU_HEAD_DIM)
        za.append(_dot(vcb[:, cols], wa_ref[hh]))
        zx.append(_dot(vcb[:, cols], wx_ref[hh]))
    r = _sigmoid(jnp.concatenate(za, axis=-1) + ba_ref[...])
    ig = _sigmoid(jnp.concatenate(zx, axis=-1) + bx_ref[...])

    z = -lam_ref[...]
    softplus = jnp.maximum(z, 0.0) + jnp.log1p(jnp.exp(-jnp.abs(z)))
    log_a = r * (-RGLRU_C * softplus)
    a = jnp.exp(log_a)
    norm = _sqrt_nonneg(-jnp.tanh(log_a) * (a * a + 1.0))
    a_ref[...] = a
    b_ref[...] = norm * (ig * vc)

    hprev = hcarry_ref[j]
    for t in range(tm // SUBLANES):
        rows = slice(t * SUBLANES, (t + 1) * SUBLANES)
        hprev = a_ref[rows, :] * hprev + b_ref[rows, :]
        b_ref[rows, :] = hprev
    hcarry_ref[j] = hprev

    pg = _dot(h, wg_ref[...])
    y = (b_ref[...] * _silu(pg)).astype(BF16)
    acc_ref[...] += _dot(y, wo_ref[...])

    @pl.when(j == nj - 1)
    def _():
        gate = mod_ref[:, 2 * d:3 * d]
        fg = fg_ref[...]
        group = 2 * NORM_GROUP

        def body(r_, carry):
            steps = [r_ * group + k for k in range(group)]
            rows = [pl.ds(pl.multiple_of(t * SUBLANES, SUBLANES), SUBLANES) for t in steps]
            inv = []
            for k in range(group):
                xn = x_ref[rows[k], :] + gate * acc_ref[rows[k], :]
                acc_ref[rows[k], :] = xn
                inv.append(_inv_rms(xn))
            for k in range(group):
                o_ref[:, steps[k], :] = acc_ref[rows[k], :] * inv[k] * fg
            return carry

        lax.fori_loop(0, tm // (group * SUBLANES), body, 0)


def _rglru_layer(xt, mod, g, w_in, conv_w, conv_b, w_a, b_a, w_x, b_x, lam, w_out,
                 final_g, *, bsz, tm, te):
    t, d = xt.shape
    e = w_out.shape[0]
    taps = conv_w.shape[0]
    nj = e // te
    ts = tm // bsz
    hps = te // LRU_HEAD_DIM
    halo = (taps - 1) * SUBLANES
    row = lambda a: a.reshape(1, e)
    cols = lambda i, j: (0, j)
    heads = lambda i, j: (j, 0, 0)
    return pl.pallas_call(
        functools.partial(_rglru_kernel, tm=tm, te=te, d=d, taps=taps, nj=nj),
        grid=(t // tm, nj),
        in_specs=[
            pl.BlockSpec((tm, d), lambda i, j: (i, 0)),
            pl.BlockSpec(mod.shape, lambda i, j: (0, 0)),
            pl.BlockSpec((1, d), lambda i, j: (0, 0)),
            pl.BlockSpec((d, te), cols),
            pl.BlockSpec((d, te), lambda i, j: (0, nj + j)),
            pl.BlockSpec((taps, te), cols),
            pl.BlockSpec((1, te), cols),
            pl.BlockSpec((hps, LRU_HEAD_DIM, LRU_HEAD_DIM), heads),
            pl.BlockSpec((1, te), cols),
            pl.BlockSpec((hps, LRU_HEAD_DIM, LRU_HEAD_DIM), heads),
            pl.BlockSpec((1, te), cols),
            pl.BlockSpec((1, te), cols),
            pl.BlockSpec((te, d), lambda i, j: (j, 0)),
            pl.BlockSpec((1, d), lambda i, j: (0, 0)),
        ],
        out_specs=pl.BlockSpec((bsz, ts, d), lambda i, j: (0, i, 0)),
        out_shape=jax.ShapeDtypeStruct((bsz, t // bsz, d), F32),
        scratch_shapes=[
            pltpu.VMEM((tm, d), F32),
            pltpu.VMEM((tm, d), BF16),
            pltpu.VMEM((tm + halo, te), F32),
            pltpu.VMEM((tm, te), F32),
            pltpu.VMEM((tm, te), F32),
            pltpu.VMEM((nj, halo, te), F32),
            pltpu.VMEM((nj, SUBLANES, te), F32),
        ],
        compiler_params=pltpu.CompilerParams(
            dimension_semantics=("arbitrary", "arbitrary"),
            vmem_limit_bytes=V7X_VMEM_LIMIT_BYTES),
        name="rglru_layer",
    )(xt, mod, g, w_in, w_in, conv_w, row(conv_b), w_a, row(b_a), w_x, row(b_x), row(lam),
      w_out, final_g.reshape(1, d))


def kernel(x, c, norm_g, ada_w, ada_b, sc_w_in, sc_conv_w, sc_w_out, lru_w_in, lru_conv_w,
           lru_conv_b, lru_w_a, lru_b_a, lru_w_x, lru_b_x, lru_lambda, lru_w_out, final_g):
    bsz, seq, d = x.shape
    assert bsz == SUBLANES, "rows are grouped so that one f32 vreg holds all sequences"
    assert ada_w.shape[0] == 2 and sc_w_in.shape[0] == 1 and lru_w_in.shape[0] == 1

    mod = _adaln_mod(c, ada_w, ada_b, tn=768)
    x1 = _sconv_layer(x, mod[0], norm_g[0:1], sc_w_in[0].astype(BF16), sc_conv_w[0],
                      sc_w_out[0].astype(BF16), tm=512, te=256)
    return _rglru_layer(x1, mod[1], norm_g[1:2], lru_w_in[0].astype(BF16), lru_conv_w[0],
                        lru_conv_b[0], lru_w_a[0].astype(BF16), lru_b_a[0],
                        lru_w_x[0].astype(BF16), lru_b_x[0], lru_lambda[0],
                        lru_w_out[0].astype(BF16), final_g, bsz=bsz, tm=512, te=512)
```

```python
import functools

import jax
import jax.numpy as jnp
from jax import lax
from jax.experimental import pallas as pl
from jax.experimental.pallas import tpu as pltpu

EPS = 1e-6
RGLRU_C = 8.0
LRU_HEAD_DIM = 256

SUBLANES = 8
PACKED_ROWS = 2 * SUBLANES
NORM_GROUP = 8
V7X_VMEM_LIMIT_BYTES = 56 * 1024 * 1024

F32 = jnp.float32
BF16 = jnp.bfloat16


def _sigmoid(z):
    return 0.5 * jnp.tanh(0.5 * z) + 0.5


def _silu(z):
    hz = 0.5 * z
    return hz * jnp.tanh(hz) + hz


def _sqrt_nonneg(x):
    return jnp.where(x > 0.0, x * lax.rsqrt(x), 0.0)


def _inv_rms(x):
    return lax.rsqrt(jnp.mean(x * x, axis=-1, keepdims=True) + EPS)


def _dot(a, b):
    return jnp.dot(a, b, preferred_element_type=F32)


def _mod_kernel(c_ref, w_ref, b_ref, o_ref):
    s = _silu(c_ref[...]).astype(BF16)
    w = w_ref[...].astype(BF16)
    o_ref[...] = _dot(s, w) + b_ref[...]


def _adaln_mod(c, ada_w, ada_b, *, tn):
    depth, d, n = ada_w.shape
    bsz = c.shape[0]
    return pl.pallas_call(
        _mod_kernel,
        grid=(depth, n // tn),
        in_specs=[
            pl.BlockSpec((bsz, d), lambda l, k: (0, 0)),
            pl.BlockSpec((None, d, tn), lambda l, k: (l, 0, k)),
            pl.BlockSpec((None, 1, tn), lambda l, k: (l, 0, k)),
        ],
        out_specs=pl.BlockSpec((None, bsz, tn), lambda l, k: (l, 0, k)),
        out_shape=jax.ShapeDtypeStruct((depth, bsz, n), F32),
        compiler_params=pltpu.CompilerParams(
            dimension_semantics=("arbitrary", "arbitrary"),
            vmem_limit_bytes=V7X_VMEM_LIMIT_BYTES),
        name="adaln_mod",
    )(c, ada_w, ada_b.reshape(depth, 1, n))


def _modulate_rows(load_block, xt_ref, mod_ref, g_ref, h_ref, *, tm, d):
    g = g_ref[...]
    shift = jnp.tile(mod_ref[:, 0:d], (2, 1))
    scale1 = 1.0 + jnp.tile(mod_ref[:, d:2 * d], (2, 1))

    def body(r, carry):
        blocks = [r * NORM_GROUP + k for k in range(NORM_GROUP)]
        rows = [pl.ds(pl.multiple_of(b * PACKED_ROWS, PACKED_ROWS), PACKED_ROWS)
                for b in blocks]
        inv = []
        for k in range(NORM_GROUP):
            x = load_block(blocks[k])
            if xt_ref is not None:
                xt_ref[rows[k], :] = x
            inv.append(_inv_rms(x))
        for k in range(NORM_GROUP):
            x = load_block(blocks[k]) if xt_ref is None else xt_ref[rows[k], :]
            h_ref[rows[k], :] = (x * inv[k] * g * scale1 + shift).astype(h_ref.dtype)
        return carry

    lax.fori_loop(0, tm // (NORM_GROUP * PACKED_ROWS), body, 0)


def _zero_rows(ref, *, tm):
    def body(r, carry):
        rows = pl.ds(pl.multiple_of(r * SUBLANES, SUBLANES), SUBLANES)
        ref[rows, :] = jnp.zeros((SUBLANES, ref.shape[1]), ref.dtype)
        return carry

    lax.fori_loop(0, tm // SUBLANES, body, 0, unroll=8)


def _causal_taps(ext_ref, cw_ref, *, tm, taps):
    acc = cw_ref[0:1, :] * ext_ref[0:tm, :]
    for k in range(1, taps):
        acc = acc + cw_ref[k:k + 1, :] * ext_ref[k * SUBLANES:k * SUBLANES + tm, :]
    return acc


def _sconv_kernel(x_ref, mod_ref, g_ref, wb_ref, wc_ref, wv_ref, wg_ref, cw_ref, wo_ref,
                  o_ref, xt_ref, h_ref, y0_ref, y1_ref, cv_ref, carry_ref,
                  *, tm, te, d, taps, nj):
    i = pl.program_id(0)
    j = pl.program_id(1)
    halo = (taps - 1) * SUBLANES

    @pl.when(j == 0)
    def _():
        @pl.when(i == 0)
        def _():
            carry_ref[...] = jnp.zeros(carry_ref.shape, F32)

        def load_block(k16):
            t0 = k16 * 2
            return jnp.concatenate([x_ref[:, t0, :], x_ref[:, t0 + 1, :]], axis=0)

        _modulate_rows(load_block, xt_ref, mod_ref, g_ref, h_ref, tm=tm, d=d)
        _zero_rows(o_ref, tm=tm)

    def step(y_mine, y_other):
        if y_mine is not None:
            h = h_ref[...]
            cv_ref[0:halo, :] = carry_ref[j]
            cv_ref[halo:halo + tm, :] = _dot(h, wc_ref[...]) * _dot(h, wv_ref[...])
            carry_ref[j] = cv_ref[tm:tm + halo, :]
            pb = _dot(h, wb_ref[...])
            pg = _dot(h, wg_ref[...])
        if y_other is not None:
            o_ref[...] += _dot(y_other[...], wo_ref[...].astype(BF16))
        if y_mine is not None:
            u = _causal_taps(cv_ref, cw_ref, tm=tm, taps=taps)
            y_mine[...] = (pb * u * _silu(pg)).astype(BF16)

    odd = lax.rem(j, 2) == 1

    @pl.when(j == 0)
    def _():
        step(y0_ref, None)

    @pl.when(jnp.logical_and(odd, j < nj))
    def _():
        step(y1_ref, y0_ref)

    @pl.when(jnp.logical_and(jnp.logical_not(odd), jnp.logical_and(j > 0, j < nj)))
    def _():
        step(y0_ref, y1_ref)

    @pl.when(j == nj)
    def _():
        step(None, y1_ref if nj % 2 == 0 else y0_ref)
        gate = mod_ref[:, 2 * d:3 * d]

        def body(r, carry):
            rows = pl.ds(pl.multiple_of(r * SUBLANES, SUBLANES), SUBLANES)
            o_ref[rows, :] = xt_ref[rows, :] + gate * o_ref[rows, :]
            return carry

        lax.fori_loop(0, tm // SUBLANES, body, 0, unroll=8)


def _sconv_layer(x, mod, g, w_in, conv_w, w_out, *, tm, te):
    bsz, seq, d = x.shape
    e = w_out.shape[0]
    taps = conv_w.shape[0]
    nj = e // te
    ts = tm // bsz
    halo = (taps - 1) * SUBLANES
    cur = lambda j: jnp.minimum(j, nj - 1)
    prev = lambda j: jnp.maximum(j - 1, 0)
    col = lambda c: (lambda i, j: (0, c * nj + cur(j)))
    return pl.pallas_call(
        functools.partial(_sconv_kernel, tm=tm, te=te, d=d, taps=taps, nj=nj),
        grid=(seq // ts, nj + 1),
        in_specs=[
            pl.BlockSpec((bsz, ts, d), lambda i, j: (0, i, 0)),
            pl.BlockSpec(mod.shape, lambda i, j: (0, 0)),
            pl.BlockSpec((1, d), lambda i, j: (0, 0)),
            pl.BlockSpec((d, te), col(0)),
            pl.BlockSpec((d, te), col(1)),
            pl.BlockSpec((d, te), col(2)),
            pl.BlockSpec((d, te), col(3)),
            pl.BlockSpec((taps, te), lambda i, j: (0, cur(j))),
            pl.BlockSpec((te, d), lambda i, j: (prev(j), 0)),
        ],
        out_specs=pl.BlockSpec((tm, d), lambda i, j: (i, 0)),
        out_shape=jax.ShapeDtypeStruct((seq * bsz, d), F32),
        scratch_shapes=[
            pltpu.VMEM((tm, d), F32),
            pltpu.VMEM((tm, d), BF16),
            pltpu.VMEM((tm, te), BF16),
            pltpu.VMEM((tm, te), BF16),
            pltpu.VMEM((tm + halo, te), F32),
            pltpu.VMEM((nj, halo, te), F32),
        ],
        compiler_params=pltpu.CompilerParams(
            dimension_semantics=("arbitrary", "arbitrary"),
            vmem_limit_bytes=V7X_VMEM_LIMIT_BYTES),
        name="sconv_layer",
    )(x, mod, g, w_in, w_in, w_in, w_in, conv_w, w_out)


def _rglru_kernel(x_ref, mod_ref, g_ref, wv_ref, wg_ref, cw_ref, cb_ref, wa_ref, ba_ref,
                  wx_ref, bx_ref, lam_ref, wo_ref, fg_ref,
                  o_ref, acc_ref, h_ref, v_ref, a_ref, b_ref, vcarry_ref, hcarry_ref,
                  *, tm, te, d, taps, nj):
    i = pl.program_id(0)
    j = pl.program_id(1)
    halo = (taps - 1) * SUBLANES

    @pl.when(j == 0)
    def _():
        @pl.when(i == 0)
        def _():
            vcarry_ref[...] = jnp.zeros(vcarry_ref.shape, F32)
            hcarry_ref[...] = jnp.zeros(hcarry_ref.shape, F32)

        def load_block(k16):
            return x_ref[pl.ds(pl.multiple_of(k16 * PACKED_ROWS, PACKED_ROWS), PACKED_ROWS), :]

        _modulate_rows(load_block, None, mod_ref, g_ref, h_ref, tm=tm, d=d)
        _zero_rows(acc_ref, tm=tm)

    h = h_ref[...]
    v_ref[0:halo, :] = vcarry_ref[j]
    v_ref[halo:halo + tm, :] = _dot(h, wv_ref[...])
    vcarry_ref[j] = v_ref[tm:tm + halo, :]
    vc = cb_ref[...] + _causal_taps(v_ref, cw_ref, tm=tm, taps=taps)

    vcb = vc.astype(BF16)
    za, zx = [], []
    for hh in range(te // LRU_HEAD_DIM):
        cols = slice(hh * LRU_HEAD_DIM, (hh + 1) * LRU_HEAD_DIM)
        za.append(_dot(vcb[:, cols], wa_ref[hh].astype(BF16)))
        zx.append(_dot(vcb[:, cols], wx_ref[hh].astype(BF16)))
    r = _sigmoid(jnp.concatenate(za, axis=-1) + ba_ref[...])
    ig = _sigmoid(jnp.concatenate(zx, axis=-1) + bx_ref[...])

    z = -lam_ref[...]
    softplus = jnp.maximum(z, 0.0) + jnp.log1p(jnp.exp(-jnp.abs(z)))
    log_a = r * (-RGLRU_C * softplus)
    a = jnp.exp(log_a)
    norm = _sqrt_nonneg(-jnp.tanh(log_a) * (a * a + 1.0))
    a_ref[...] = a
    b_ref[...] = norm * (ig * vc)

    hprev = hcarry_ref[j]
    for t in range(tm // SUBLANES):
        rows = slice(t * SUBLANES, (t + 1) * SUBLANES)
        hprev = a_ref[rows, :] * hprev + b_ref[rows, :]
        b_ref[rows, :] = hprev
    hcarry_ref[j] = hprev

    pg = _dot(h, wg_ref[...])
    y = (b_ref[...] * _silu(pg)).astype(BF16)
    acc_ref[...] += _dot(y, wo_ref[...].astype(BF16))

    @pl.when(j == nj - 1)
    def _():
        gate = mod_ref[:, 2 * d:3 * d]
        fg = fg_ref[...]
        group = 2 * NORM_GROUP

        def body(r_, carry):
            steps = [r_ * group + k for k in range(group)]
            rows = [pl.ds(pl.multiple_of(t * SUBLANES, SUBLANES), SUBLANES) for t in steps]
            inv = []
            for k in range(group):
                xn = x_ref[rows[k], :] + gate * acc_ref[rows[k], :]
                acc_ref[rows[k], :] = xn
                inv.append(_inv_rms(xn))
            for k in range(group):
                o_ref[:, steps[k], :] = acc_ref[rows[k], :] * inv[k] * fg
            return carry

        lax.fori_loop(0, tm // (group * SUBLANES), body, 0)


def _rglru_layer(xt, mod, g, w_in, conv_w, conv_b, w_a, b_a, w_x, b_x, lam, w_out,
                 final_g, *, bsz, tm, te):
    t, d = xt.shape
    e = w_out.shape[0]
    taps = conv_w.shape[0]
    nj = e // te
    ts = tm // bsz
    hps = te // LRU_HEAD_DIM
    halo = (taps - 1) * SUBLANES
    row = lambda a: a.reshape(1, e)
    cols = lambda i, j: (0, j)
    heads = lambda i, j: (j, 0, 0)
    return pl.pallas_call(
        functools.partial(_rglru_kernel, tm=tm, te=te, d=d, taps=taps, nj=nj),
        grid=(t // tm, nj),
        in_specs=[
            pl.BlockSpec((tm, d), lambda i, j: (i, 0)),
            pl.BlockSpec(mod.shape, lambda i, j: (0, 0)),
            pl.BlockSpec((1, d), lambda i, j: (0, 0)),
            pl.BlockSpec((d, te), cols),
            pl.BlockSpec((d, te), lambda i, j: (0, nj + j)),
            pl.BlockSpec((taps, te), cols),
            pl.BlockSpec((1, te), cols),
            pl.BlockSpec((hps, LRU_HEAD_DIM, LRU_HEAD_DIM), heads),
            pl.BlockSpec((1, te), cols),
            pl.BlockSpec((hps, LRU_HEAD_DIM, LRU_HEAD_DIM), heads),
            pl.BlockSpec((1, te), cols),
            pl.BlockSpec((1, te), cols),
            pl.BlockSpec((te, d), lambda i, j: (j, 0)),
            pl.BlockSpec((1, d), lambda i, j: (0, 0)),
        ],
        out_specs=pl.BlockSpec((bsz, ts, d), lambda i, j: (0, i, 0)),
        out_shape=jax.ShapeDtypeStruct((bsz, t // bsz, d), F32),
        scratch_shapes=[
            pltpu.VMEM((tm, d), F32),
            pltpu.VMEM((tm, d), BF16),
            pltpu.VMEM((tm + halo, te), F32),
            pltpu.VMEM((tm, te), F32),
            pltpu.VMEM((tm, te), F32),
            pltpu.VMEM((nj, halo, te), F32),
            pltpu.VMEM((nj, SUBLANES, te), F32),
        ],
        compiler_params=pltpu.CompilerParams(
            dimension_semantics=("arbitrary", "arbitrary"),
            vmem_limit_bytes=V7X_VMEM_LIMIT_BYTES),
        name="rglru_layer",
    )(xt, mod, g, w_in, w_in, conv_w, row(conv_b), w_a, row(b_a), w_x, row(b_x), row(lam),
      w_out, final_g.reshape(1, d))


def kernel(x, c, norm_g, ada_w, ada_b, sc_w_in, sc_conv_w, sc_w_out, lru_w_in, lru_conv_w,
           lru_conv_b, lru_w_a, lru_b_a, lru_w_x, lru_b_x, lru_lambda, lru_w_out, final_g):
    bsz, seq, d = x.shape
    assert bsz == SUBLANES, "rows are grouped so that one f32 vreg holds all sequences"
    assert ada_w.shape[0] == 2 and sc_w_in.shape[0] == 1 and lru_w_in.shape[0] == 1

    mod = _adaln_mod(c, ada_w, ada_b, tn=768)
    x1 = _sconv_layer(x, mod[0], norm_g[0:1], sc_w_in[0].astype(BF16), sc_conv_w[0],
                      sc_w_out[0], tm=512, te=256)
    return _rglru_layer(x1, mod[1], norm_g[1:2], lru_w_in[0].astype(BF16), lru_conv_w[0],
                        lru_conv_b[0], lru_w_a[0], lru_b_a[0], lru_w_x[0], lru_b_x[0],
                        lru_lambda[0], lru_w_out[0], final_g, bsz=bsz, tm=512, te=512)
```

```python
import functools

import jax
import jax.numpy as jnp
from jax import lax
from jax.experimental import pallas as pl
from jax.experimental.pallas import tpu as pltpu

EPS = 1e-6
RGLRU_C = 8.0
LRU_HEAD_DIM = 256

SUBLANES = 8
PACKED_ROWS = 2 * SUBLANES
NORM_GROUP = 8
V7X_VMEM_LIMIT_BYTES = 56 * 1024 * 1024

F32 = jnp.float32
BF16 = jnp.bfloat16


def _sigmoid(z):
    return 0.5 * jnp.tanh(0.5 * z) + 0.5


def _silu(z):
    hz = 0.5 * z
    return hz * jnp.tanh(hz) + hz


def _sqrt_nonneg(x):
    return jnp.where(x > 0.0, x * lax.rsqrt(x), 0.0)


def _inv_rms(x):
    return lax.rsqrt(jnp.mean(x * x, axis=-1, keepdims=True) + EPS)


def _dot(a, b):
    return jnp.dot(a, b, preferred_element_type=F32)


def _mod_kernel(c_ref, w_ref, b_ref, o_ref):
    s = _silu(c_ref[...]).astype(BF16)
    w = w_ref[...].astype(BF16)
    o_ref[...] = _dot(s, w) + b_ref[...]


def _adaln_mod(c, ada_w, ada_b, *, tn):
    depth, d, n = ada_w.shape
    bsz = c.shape[0]
    return pl.pallas_call(
        _mod_kernel,
        grid=(depth, n // tn),
        in_specs=[
            pl.BlockSpec((bsz, d), lambda l, k: (0, 0)),
            pl.BlockSpec((None, d, tn), lambda l, k: (l, 0, k)),
            pl.BlockSpec((None, 1, tn), lambda l, k: (l, 0, k)),
        ],
        out_specs=pl.BlockSpec((None, bsz, tn), lambda l, k: (l, 0, k)),
        out_shape=jax.ShapeDtypeStruct((depth, bsz, n), F32),
        compiler_params=pltpu.CompilerParams(
            dimension_semantics=("arbitrary", "arbitrary"),
            vmem_limit_bytes=V7X_VMEM_LIMIT_BYTES),
        name="adaln_mod",
    )(c, ada_w, ada_b.reshape(depth, 1, n))


def _modulate_rows(load_block, xt_ref, mod_ref, g_ref, h_ref, *, tm, d):
    g = g_ref[...]
    shift = jnp.tile(mod_ref[:, 0:d], (2, 1))
    scale1 = 1.0 + jnp.tile(mod_ref[:, d:2 * d], (2, 1))

    def body(r, carry):
        blocks = [r * NORM_GROUP + k for k in range(NORM_GROUP)]
        rows = [pl.ds(pl.multiple_of(b * PACKED_ROWS, PACKED_ROWS), PACKED_ROWS)
                for b in blocks]
        inv = []
        for k in range(NORM_GROUP):
            x = load_block(blocks[k])
            if xt_ref is not None:
                xt_ref[rows[k], :] = x
            inv.append(_inv_rms(x))
        for k in range(NORM_GROUP):
            x = load_block(blocks[k]) if xt_ref is None else xt_ref[rows[k], :]
            h_ref[rows[k], :] = (x * inv[k] * g * scale1 + shift).astype(h_ref.dtype)
        return carry

    lax.fori_loop(0, tm // (NORM_GROUP * PACKED_ROWS), body, 0)


def _zero_rows(ref, *, tm):
    def body(r, carry):
        rows = pl.ds(pl.multiple_of(r * SUBLANES, SUBLANES), SUBLANES)
        ref[rows, :] = jnp.zeros((SUBLANES, ref.shape[1]), ref.dtype)
        return carry

    lax.fori_loop(0, tm // SUBLANES, body, 0, unroll=8)


def _causal_taps(ext_ref, cw_ref, *, tm, taps):
    acc = cw_ref[0:1, :] * ext_ref[0:tm, :]
    for k in range(1, taps):
        acc = acc + cw_ref[k:k + 1, :] * ext_ref[k * SUBLANES:k * SUBLANES + tm, :]
    return acc


def _sconv_kernel(x_ref, mod_ref, g_ref, wb_ref, wc_ref, wv_ref, wg_ref, cw_ref, wo_ref,
                  o_ref, h_ref, cv_ref, carry_ref, *, tm, te, d, taps):
    i = pl.program_id(0)
    j = pl.program_id(1)
    halo = (taps - 1) * SUBLANES

    @pl.when(j == 0)
    def _():
        @pl.when(i == 0)
        def _():
            carry_ref[...] = jnp.zeros(carry_ref.shape, F32)

        def load_block(k16):
            t0 = k16 * 2
            return jnp.concatenate([x_ref[:, t0, :], x_ref[:, t0 + 1, :]], axis=0)

        _modulate_rows(load_block, o_ref, mod_ref, g_ref, h_ref, tm=tm, d=d)

    h = h_ref[...]
    cv_ref[0:halo, :] = carry_ref[j]
    cv_ref[halo:halo + tm, :] = _dot(h, wc_ref[...]) * _dot(h, wv_ref[...])
    carry_ref[j] = cv_ref[tm:tm + halo, :]
    u = _causal_taps(cv_ref, cw_ref, tm=tm, taps=taps)
    y = (_dot(h, wb_ref[...]) * u * _silu(_dot(h, wg_ref[...]))).astype(BF16)
    gate = mod_ref[:, 2 * d:3 * d]
    upd = _dot(y, wo_ref[...]).reshape(tm // SUBLANES, SUBLANES, d) * gate[None]
    o_ref[...] += upd.reshape(tm, d)


def _sconv_layer(x, mod, g, w_in, conv_w, w_out, *, tm, te):
    bsz, seq, d = x.shape
    e = w_out.shape[0]
    taps = conv_w.shape[0]
    nj = e // te
    ts = tm // bsz
    halo = (taps - 1) * SUBLANES
    col = lambda c: (lambda i, j: (0, c * nj + j))
    return pl.pallas_call(
        functools.partial(_sconv_kernel, tm=tm, te=te, d=d, taps=taps),
        grid=(seq // ts, nj),
        in_specs=[
            pl.BlockSpec((bsz, ts, d), lambda i, j: (0, i, 0)),
            pl.BlockSpec(mod.shape, lambda i, j: (0, 0)),
            pl.BlockSpec((1, d), lambda i, j: (0, 0)),
            pl.BlockSpec((d, te), col(0)),
            pl.BlockSpec((d, te), col(1)),
            pl.BlockSpec((d, te), col(2)),
            pl.BlockSpec((d, te), col(3)),
            pl.BlockSpec((taps, te), lambda i, j: (0, j)),
            pl.BlockSpec((te, d), lambda i, j: (j, 0)),
        ],
        out_specs=pl.BlockSpec((tm, d), lambda i, j: (i, 0)),
        out_shape=jax.ShapeDtypeStruct((seq * bsz, d), F32),
        scratch_shapes=[
            pltpu.VMEM((tm, d), BF16),
            pltpu.VMEM((tm + halo, te), F32),
            pltpu.VMEM((nj, halo, te), F32),
        ],
        compiler_params=pltpu.CompilerParams(
            dimension_semantics=("arbitrary", "arbitrary"),
            vmem_limit_bytes=V7X_VMEM_LIMIT_BYTES),
        name="sconv_layer",
    )(x, mod, g, w_in, w_in, w_in, w_in, conv_w, w_out)


def _rglru_kernel(x_ref, mod_ref, g_ref, wv_ref, wg_ref, p_ref, wa_ref, wx_ref, wo_ref, fg_ref,
                  o_ref, acc_ref, h_ref, v_ref, a_ref, b_ref, vcarry_ref, hcarry_ref,
                  *, tm, te, d, taps, nj):
    i = pl.program_id(0)
    j = pl.program_id(1)
    halo = (taps - 1) * SUBLANES
    cb, ba, bx, lam = (p_ref[taps + k:taps + k + 1, :] for k in range(4))

    @pl.when(j == 0)
    def _():
        @pl.when(i == 0)
        def _():
            vcarry_ref[...] = jnp.zeros(vcarry_ref.shape, F32)
            hcarry_ref[...] = jnp.zeros(hcarry_ref.shape, F32)

        def load_block(k16):
            return x_ref[pl.ds(pl.multiple_of(k16 * PACKED_ROWS, PACKED_ROWS), PACKED_ROWS), :]

        _modulate_rows(load_block, None, mod_ref, g_ref, h_ref, tm=tm, d=d)
        _zero_rows(acc_ref, tm=tm)

    h = h_ref[...]
    v_ref[0:halo, :] = vcarry_ref[j]
    v_ref[halo:halo + tm, :] = _dot(h, wv_ref[...])
    vcarry_ref[j] = v_ref[tm:tm + halo, :]
    vc = cb + _causal_taps(v_ref, p_ref, tm=tm, taps=taps)

    vcb = vc.astype(BF16)
    za, zx = [], []
    for hh in range(te // LRU_HEAD_DIM):
        cols = slice(hh * LRU_HEAD_DIM, (hh + 1) * LRU_HEAD_DIM)
        za.append(_dot(vcb[:, cols], wa_ref[hh].astype(BF16)))
        zx.append(_dot(vcb[:, cols], wx_ref[hh].astype(BF16)))
    r = _sigmoid(jnp.concatenate(za, axis=-1) + ba)
    ig = _sigmoid(jnp.concatenate(zx, axis=-1) + bx)

    z = -lam
    softplus = jnp.maximum(z, 0.0) + jnp.log1p(jnp.exp(-jnp.abs(z)))
    log_a = r * (-RGLRU_C * softplus)
    a = jnp.exp(log_a)
    norm = _sqrt_nonneg(-jnp.tanh(log_a) * (a * a + 1.0))
    a_ref[...] = a
    b_ref[...] = norm * (ig * vc)

    hprev = hcarry_ref[j]
    for t in range(tm // SUBLANES):
        rows = slice(t * SUBLANES, (t + 1) * SUBLANES)
        hprev = a_ref[rows, :] * hprev + b_ref[rows, :]
        b_ref[rows, :] = hprev
    hcarry_ref[j] = hprev

    pg = _dot(h, wg_ref[...])
    y = (b_ref[...] * _silu(pg)).astype(BF16)
    acc_ref[...] += _dot(y, wo_ref[...].astype(BF16))

    @pl.when(j == nj - 1)
    def _():
        gate = mod_ref[:, 2 * d:3 * d]
        fg = fg_ref[...]
        group = 2 * NORM_GROUP

        def body(r_, carry):
            steps = [r_ * group + k for k in range(group)]
            rows = [pl.ds(pl.multiple_of(t * SUBLANES, SUBLANES), SUBLANES) for t in steps]
            inv = []
            for k in range(group):
                xn = x_ref[rows[k], :] + gate * acc_ref[rows[k], :]
                acc_ref[rows[k], :] = xn
                inv.append(_inv_rms(xn))
            for k in range(group):
                o_ref[:, steps[k], :] = acc_ref[rows[k], :] * inv[k] * fg
            return carry

        lax.fori_loop(0, tm // (group * SUBLANES), body, 0)


def _rglru_layer(xt, mod, g, w_in, conv_w, conv_b, w_a, b_a, w_x, b_x, lam, w_out,
                 final_g, *, bsz, tm, te):
    t, d = xt.shape
    e = w_out.shape[0]
    taps = conv_w.shape[0]
    nj = e // te
    ts = tm // bsz
    hps = te // LRU_HEAD_DIM
    halo = (taps - 1) * SUBLANES
    row = lambda a: a.reshape(1, e)
    params = jnp.concatenate([conv_w, row(conv_b), row(b_a), row(b_x), row(lam)], axis=0)
    cols = lambda i, j: (0, j)
    heads = lambda i, j: (j, 0, 0)
    return pl.pallas_call(
        functools.partial(_rglru_kernel, tm=tm, te=te, d=d, taps=taps, nj=nj),
        grid=(t // tm, nj),
        in_specs=[
            pl.BlockSpec((tm, d), lambda i, j: (i, 0)),
            pl.BlockSpec(mod.shape, lambda i, j: (0, 0)),
            pl.BlockSpec((1, d), lambda i, j: (0, 0)),
            pl.BlockSpec((d, te), cols),
            pl.BlockSpec((d, te), lambda i, j: (0, nj + j)),
            pl.BlockSpec((taps + 4, te), cols),
            pl.BlockSpec((hps, LRU_HEAD_DIM, LRU_HEAD_DIM), heads),
            pl.BlockSpec((hps, LRU_HEAD_DIM, LRU_HEAD_DIM), heads),
            pl.BlockSpec((te, d), lambda i, j: (j, 0)),
            pl.BlockSpec((1, d), lambda i, j: (0, 0)),
        ],
        out_specs=pl.BlockSpec((bsz, ts, d), lambda i, j: (0, i, 0)),
        out_shape=jax.ShapeDtypeStruct((bsz, t // bsz, d), F32),
        scratch_shapes=[
            pltpu.VMEM((tm, d), F32),
            pltpu.VMEM((tm, d), BF16),
            pltpu.VMEM((tm + halo, te), F32),
            pltpu.VMEM((tm, te), F32),
            pltpu.VMEM((tm, te), F32),
            pltpu.VMEM((nj, halo, te), F32),
            pltpu.VMEM((nj, SUBLANES, te), F32),
        ],
        compiler_params=pltpu.CompilerParams(
            dimension_semantics=("arbitrary", "arbitrary"),
            vmem_limit_bytes=V7X_VMEM_LIMIT_BYTES),
        name="rglru_layer",
    )(xt, mod, g, w_in, w_in, params, w_a, w_x, w_out, final_g.reshape(1, d))


def kernel(x, c, norm_g, ada_w, ada_b, sc_w_in, sc_conv_w, sc_w_out, lru_w_in, lru_conv_w,
           lru_conv_b, lru_w_a, lru_b_a, lru_w_x, lru_b_x, lru_lambda, lru_w_out, final_g):
    bsz, seq, d = x.shape
    assert bsz == SUBLANES, "rows are grouped so that one f32 vreg holds all sequences"
    assert ada_w.shape[0] == 2 and sc_w_in.shape[0] == 1 and lru_w_in.shape[0] == 1

    mod = _adaln_mod(c, ada_w, ada_b, tn=768)
    x1 = _sconv_layer(x, mod[0], norm_g[0:1], sc_w_in[0].astype(BF16), sc_conv_w[0],
                      sc_w_out[0].astype(BF16), tm=1024, te=256)
    return _rglru_layer(x1, mod[1], norm_g[1:2], lru_w_in[0].astype(BF16), lru_conv_w[0],
                        lru_conv_b[0], lru_w_a[0], lru_b_a[0], lru_w_x[0], lru_b_x[0],
                        lru_lambda[0], lru_w_out[0], final_g, bsz=bsz, tm=512, te=512)
```

```python
import functools

import jax
import jax.numpy as jnp
from jax import lax
from jax.experimental import pallas as pl
from jax.experimental.pallas import tpu as pltpu

EPS = 1e-6
RGLRU_C = 8.0
LRU_HEAD_DIM = 256

SUBLANES = 8
PACKED_ROWS = 2 * SUBLANES
NORM_GROUP = 8
V7X_VMEM_LIMIT_BYTES = 56 * 1024 * 1024

F32 = jnp.float32
BF16 = jnp.bfloat16


def _sigmoid(z):
    return 0.5 * jnp.tanh(0.5 * z) + 0.5


def _silu(z):
    hz = 0.5 * z
    return hz * jnp.tanh(hz) + hz


def _sqrt_nonneg(x):
    return jnp.where(x > 0.0, x * lax.rsqrt(x), 0.0)


def _inv_rms(x):
    return lax.rsqrt(jnp.mean(x * x, axis=-1, keepdims=True) + EPS)


def _dot(a, b):
    return jnp.dot(a, b, preferred_element_type=F32)


def _mod_kernel(c_ref, w_ref, b_ref, o_ref):
    s = _silu(c_ref[...]).astype(BF16)
    w = w_ref[...].astype(BF16)
    o_ref[...] = _dot(s, w) + b_ref[...]


def _adaln_mod(c, ada_w, ada_b, *, tn):
    depth, d, n = ada_w.shape
    bsz = c.shape[0]
    return pl.pallas_call(
        _mod_kernel,
        grid=(depth, n // tn),
        in_specs=[
            pl.BlockSpec((bsz, d), lambda l, k: (0, 0)),
            pl.BlockSpec((None, d, tn), lambda l, k: (l, 0, k)),
            pl.BlockSpec((None, 1, tn), lambda l, k: (l, 0, k)),
        ],
        out_specs=pl.BlockSpec((None, bsz, tn), lambda l, k: (l, 0, k)),
        out_shape=jax.ShapeDtypeStruct((depth, bsz, n), F32),
        compiler_params=pltpu.CompilerParams(
            dimension_semantics=("arbitrary", "arbitrary"),
            vmem_limit_bytes=V7X_VMEM_LIMIT_BYTES),
        name="adaln_mod",
    )(c, ada_w, ada_b.reshape(depth, 1, n))


def _modulate_rows(load_block, xt_ref, mod_ref, g_ref, h_ref, *, tm, d):
    g = g_ref[...]
    shift = jnp.tile(mod_ref[:, 0:d], (2, 1))
    scale1 = 1.0 + jnp.tile(mod_ref[:, d:2 * d], (2, 1))

    def body(r, carry):
        blocks = [r * NORM_GROUP + k for k in range(NORM_GROUP)]
        rows = [pl.ds(pl.multiple_of(b * PACKED_ROWS, PACKED_ROWS), PACKED_ROWS)
                for b in blocks]
        inv = []
        for k in range(NORM_GROUP):
            x = load_block(blocks[k])
            if xt_ref is not None:
                xt_ref[rows[k], :] = x
            inv.append(_inv_rms(x))
        for k in range(NORM_GROUP):
            x = load_block(blocks[k]) if xt_ref is None else xt_ref[rows[k], :]
            h_ref[rows[k], :] = (x * inv[k] * g * scale1 + shift).astype(h_ref.dtype)
        return carry

    lax.fori_loop(0, tm // (NORM_GROUP * PACKED_ROWS), body, 0)


def _column_blocks(w, te):
    d, n = w.shape
    return jnp.transpose(w.astype(BF16).reshape(d, n // te, te), (1, 0, 2))


def _zero_rows(ref, *, tm):
    def body(r, carry):
        rows = pl.ds(pl.multiple_of(r * SUBLANES, SUBLANES), SUBLANES)
        ref[rows, :] = jnp.zeros((SUBLANES, ref.shape[1]), ref.dtype)
        return carry

    lax.fori_loop(0, tm // SUBLANES, body, 0, unroll=8)


def _causal_taps(ext_ref, cw_ref, *, tm, taps):
    acc = cw_ref[0:1, :] * ext_ref[0:tm, :]
    for k in range(1, taps):
        acc = acc + cw_ref[k:k + 1, :] * ext_ref[k * SUBLANES:k * SUBLANES + tm, :]
    return acc


def _sconv_kernel(x_ref, mod_ref, g_ref, wb_ref, wc_ref, wv_ref, wg_ref, cw_ref, wo_ref,
                  o_ref, h_ref, cv_ref, carry_ref, *, tm, te, d, taps):
    i = pl.program_id(0)
    j = pl.program_id(1)
    halo = (taps - 1) * SUBLANES

    @pl.when(j == 0)
    def _():
        @pl.when(i == 0)
        def _():
            carry_ref[...] = jnp.zeros(carry_ref.shape, F32)

        def load_block(k16):
            t0 = k16 * 2
            return jnp.concatenate([x_ref[:, t0, :], x_ref[:, t0 + 1, :]], axis=0)

        _modulate_rows(load_block, o_ref, mod_ref, g_ref, h_ref, tm=tm, d=d)

    h = h_ref[...]
    cv_ref[0:halo, :] = carry_ref[j]
    cv_ref[halo:halo + tm, :] = _dot(h, wc_ref[...]) * _dot(h, wv_ref[...])
    carry_ref[j] = cv_ref[tm:tm + halo, :]
    u = _causal_taps(cv_ref, cw_ref, tm=tm, taps=taps)
    y = (_dot(h, wb_ref[...]) * u * _silu(_dot(h, wg_ref[...]))).astype(BF16)
    gate = mod_ref[:, 2 * d:3 * d]
    upd = _dot(y, wo_ref[...]).reshape(tm // SUBLANES, SUBLANES, d) * gate[None]
    o_ref[...] += upd.reshape(tm, d)


def _sconv_layer(x, mod, g, w_in, conv_w, w_out, *, tm, te):
    bsz, seq, d = x.shape
    e = w_out.shape[0]
    taps = conv_w.shape[0]
    nj = e // te
    ts = tm // bsz
    halo = (taps - 1) * SUBLANES
    w_blk = _column_blocks(w_in, te)
    col = lambda c: pl.BlockSpec((None, d, te), lambda i, j: (c * nj + j, 0, 0))
    return pl.pallas_call(
        functools.partial(_sconv_kernel, tm=tm, te=te, d=d, taps=taps),
        grid=(seq // ts, nj),
        in_specs=[
            pl.BlockSpec((bsz, ts, d), lambda i, j: (0, i, 0)),
            pl.BlockSpec(mod.shape, lambda i, j: (0, 0)),
            pl.BlockSpec((1, d), lambda i, j: (0, 0)),
            col(0),
            col(1),
            col(2),
            col(3),
            pl.BlockSpec((taps, te), lambda i, j: (0, j)),
            pl.BlockSpec((te, d), lambda i, j: (j, 0)),
        ],
        out_specs=pl.BlockSpec((tm, d), lambda i, j: (i, 0)),
        out_shape=jax.ShapeDtypeStruct((seq * bsz, d), F32),
        scratch_shapes=[
            pltpu.VMEM((tm, d), BF16),
            pltpu.VMEM((tm + halo, te), F32),
            pltpu.VMEM((nj, halo, te), F32),
        ],
        compiler_params=pltpu.CompilerParams(
            dimension_semantics=("arbitrary", "arbitrary"),
            vmem_limit_bytes=V7X_VMEM_LIMIT_BYTES),
        name="sconv_layer",
    )(x, mod, g, w_blk, w_blk, w_blk, w_blk, conv_w, w_out)


def _rglru_kernel(x_ref, mod_ref, g_ref, wv_ref, wg_ref, p_ref, wa_ref, wx_ref, wo_ref, fg_ref,
                  o_ref, acc_ref, h_ref, v_ref, a_ref, b_ref, vcarry_ref, hcarry_ref,
                  *, tm, te, d, taps, nj):
    i = pl.program_id(0)
    j = pl.program_id(1)
    halo = (taps - 1) * SUBLANES
    cb, ba, bx, lam = (p_ref[taps + k:taps + k + 1, :] for k in range(4))

    @pl.when(j == 0)
    def _():
        @pl.when(i == 0)
        def _():
            vcarry_ref[...] = jnp.zeros(vcarry_ref.shape, F32)
            hcarry_ref[...] = jnp.zeros(hcarry_ref.shape, F32)

        def load_block(k16):
            return x_ref[pl.ds(pl.multiple_of(k16 * PACKED_ROWS, PACKED_ROWS), PACKED_ROWS), :]

        _modulate_rows(load_block, None, mod_ref, g_ref, h_ref, tm=tm, d=d)
        _zero_rows(acc_ref, tm=tm)

    h = h_ref[...]
    v_ref[0:halo, :] = vcarry_ref[j]
    v_ref[halo:halo + tm, :] = _dot(h, wv_ref[...])
    vcarry_ref[j] = v_ref[tm:tm + halo, :]
    vc = cb + _causal_taps(v_ref, p_ref, tm=tm, taps=taps)

    vcb = vc.astype(BF16)
    za, zx = [], []
    for hh in range(te // LRU_HEAD_DIM):
        cols = slice(hh * LRU_HEAD_DIM, (hh + 1) * LRU_HEAD_DIM)
        za.append(_dot(vcb[:, cols], wa_ref[hh].astype(BF16)))
        zx.append(_dot(vcb[:, cols], wx_ref[hh].astype(BF16)))
    r = _sigmoid(jnp.concatenate(za, axis=-1) + ba)
    ig = _sigmoid(jnp.concatenate(zx, axis=-1) + bx)

    z = -lam
    softplus = jnp.maximum(z, 0.0) + jnp.log1p(jnp.exp(-jnp.abs(z)))
    log_a = r * (-RGLRU_C * softplus)
    a = jnp.exp(log_a)
    norm = _sqrt_nonneg(-jnp.tanh(log_a) * (a * a + 1.0))
    a_ref[...] = a
    b_ref[...] = norm * (ig * vc)

    hprev = hcarry_ref[j]
    for t in range(tm // SUBLANES):
        rows = slice(t * SUBLANES, (t + 1) * SUBLANES)
        hprev = a_ref[rows, :] * hprev + b_ref[rows, :]
        b_ref[rows, :] = hprev
    hcarry_ref[j] = hprev

    pg = _dot(h, wg_ref[...])
    y = (b_ref[...] * _silu(pg)).astype(BF16)
    acc_ref[...] += _dot(y, wo_ref[...].astype(BF16))

    @pl.when(j == nj - 1)
    def _():
        gate = mod_ref[:, 2 * d:3 * d]
        fg = fg_ref[...]
        group = 2 * NORM_GROUP

        def body(r_, carry):
            steps = [r_ * group + k for k in range(group)]
            rows = [pl.ds(pl.multiple_of(t * SUBLANES, SUBLANES), SUBLANES) for t in steps]
            inv = []
            for k in range(group):
                xn = x_ref[rows[k], :] + gate * acc_ref[rows[k], :]
                acc_ref[rows[k], :] = xn
                inv.append(_inv_rms(xn))
            for k in range(group):
                o_ref[:, steps[k], :] = acc_ref[rows[k], :] * inv[k] * fg
            return carry

        lax.fori_loop(0, tm // (group * SUBLANES), body, 0)


def _rglru_layer(xt, mod, g, w_in, conv_w, conv_b, w_a, b_a, w_x, b_x, lam, w_out,
                 final_g, *, bsz, tm, te):
    t, d = xt.shape
    e = w_out.shape[0]
    taps = conv_w.shape[0]
    nj = e // te
    ts = tm // bsz
    hps = te // LRU_HEAD_DIM
    halo = (taps - 1) * SUBLANES
    row = lambda a: a.reshape(1, e)
    params = jnp.concatenate([conv_w, row(conv_b), row(b_a), row(b_x), row(lam)], axis=0)
    w_blk = _column_blocks(w_in, te)
    cols = lambda i, j: (0, j)
    heads = lambda i, j: (j, 0, 0)
    return pl.pallas_call(
        functools.partial(_rglru_kernel, tm=tm, te=te, d=d, taps=taps, nj=nj),
        grid=(t // tm, nj),
        in_specs=[
            pl.BlockSpec((tm, d), lambda i, j: (i, 0)),
            pl.BlockSpec(mod.shape, lambda i, j: (0, 0)),
            pl.BlockSpec((1, d), lambda i, j: (0, 0)),
            pl.BlockSpec((None, d, te), lambda i, j: (j, 0, 0)),
            pl.BlockSpec((None, d, te), lambda i, j: (nj + j, 0, 0)),
            pl.BlockSpec((taps + 4, te), cols),
            pl.BlockSpec((hps, LRU_HEAD_DIM, LRU_HEAD_DIM), heads),
            pl.BlockSpec((hps, LRU_HEAD_DIM, LRU_HEAD_DIM), heads),
            pl.BlockSpec((te, d), lambda i, j: (j, 0)),
            pl.BlockSpec((1, d), lambda i, j: (0, 0)),
        ],
        out_specs=pl.BlockSpec((bsz, ts, d), lambda i, j: (0, i, 0)),
        out_shape=jax.ShapeDtypeStruct((bsz, t // bsz, d), F32),
        scratch_shapes=[
            pltpu.VMEM((tm, d), F32),
            pltpu.VMEM((tm, d), BF16),
            pltpu.VMEM((tm + halo, te), F32),
            pltpu.VMEM((tm, te), F32),
            pltpu.VMEM((tm, te), F32),
            pltpu.VMEM((nj, halo, te), F32),
            pltpu.VMEM((nj, SUBLANES, te), F32),
        ],
        compiler_params=pltpu.CompilerParams(
            dimension_semantics=("arbitrary", "arbitrary"),
            vmem_limit_bytes=V7X_VMEM_LIMIT_BYTES),
        name="rglru_layer",
    )(xt, mod, g, w_blk, w_blk, params, w_a, w_x, w_out, final_g.reshape(1, d))


def kernel(x, c, norm_g, ada_w, ada_b, sc_w_in, sc_conv_w, sc_w_out, lru_w_in, lru_conv_w,
           lru_conv_b, lru_w_a, lru_b_a, lru_w_x, lru_b_x, lru_lambda, lru_w_out, final_g):
    bsz, seq, d = x.shape
    assert bsz == SUBLANES, "rows are grouped so that one f32 vreg holds all sequences"
    assert ada_w.shape[0] == 2 and sc_w_in.shape[0] == 1 and lru_w_in.shape[0] == 1

    mod = _adaln_mod(c, ada_w, ada_b, tn=768)
    x1 = _sconv_layer(x, mod[0], norm_g[0:1], sc_w_in[0], sc_conv_w[0],
                      sc_w_out[0].astype(BF16), tm=1024, te=256)
    return _rglru_layer(x1, mod[1], norm_g[1:2], lru_w_in[0], lru_conv_w[0],
                        lru_conv_b[0], lru_w_a[0], lru_b_a[0], lru_w_x[0], lru_b_x[0],
                        lru_lambda[0], lru_w_out[0], final_g, bsz=bsz, tm=512, te=512)
```

```python
import functools

import jax
import jax.numpy as jnp
from jax import lax
from jax.experimental import pallas as pl
from jax.experimental.pallas import tpu as pltpu

EPS = 1e-6
RGLRU_C = 8.0
LRU_HEAD_DIM = 256

SUBLANES = 8
PACKED_ROWS = 2 * SUBLANES
NORM_GROUP = 8
V7X_VMEM_LIMIT_BYTES = 56 * 1024 * 1024

SCONV_TILE = (1024, 256)
RGLRU_TILE = (512, 512)
ADALN_COLS = 768

F32 = jnp.float32
BF16 = jnp.bfloat16


def _sigmoid(z):
    return 0.5 * jnp.tanh(0.5 * z) + 0.5


def _silu(z):
    hz = 0.5 * z
    return hz * jnp.tanh(hz) + hz


def _sqrt_nonneg(x):
    return jnp.where(x > 0.0, x * lax.rsqrt(x), 0.0)


def _inv_rms(x):
    return lax.rsqrt(jnp.mean(x * x, axis=-1, keepdims=True) + EPS)


def _dot(a, b):
    return jnp.dot(a, b, preferred_element_type=F32)


def _mod_kernel(c_ref, w_ref, b_ref, o_ref):
    s = _silu(c_ref[...]).astype(BF16)
    w = w_ref[...].astype(BF16)
    o_ref[...] = _dot(s, w) + b_ref[...]


def _adaln_mod(c, ada_w, ada_b, *, tn):
    depth, d, n = ada_w.shape
    bsz = c.shape[0]
    return pl.pallas_call(
        _mod_kernel,
        grid=(depth, n // tn),
        in_specs=[
            pl.BlockSpec((bsz, d), lambda l, k: (0, 0)),
            pl.BlockSpec((None, d, tn), lambda l, k: (l, 0, k)),
            pl.BlockSpec((None, 1, tn), lambda l, k: (l, 0, k)),
        ],
        out_specs=pl.BlockSpec((None, bsz, tn), lambda l, k: (l, 0, k)),
        out_shape=jax.ShapeDtypeStruct((depth, bsz, n), F32),
        compiler_params=pltpu.CompilerParams(
            dimension_semantics=("arbitrary", "arbitrary"),
            vmem_limit_bytes=V7X_VMEM_LIMIT_BYTES),
        name="adaln_mod",
    )(c, ada_w, ada_b.reshape(depth, 1, n))


def _modulate_rows(load_block, xt_ref, mod_ref, g_ref, h_ref, *, tm, d):
    g = g_ref[...]
    shift = jnp.tile(mod_ref[:, 0:d], (2, 1))
    scale1 = 1.0 + jnp.tile(mod_ref[:, d:2 * d], (2, 1))

    def body(r, carry):
        blocks = [r * NORM_GROUP + k for k in range(NORM_GROUP)]
        rows = [pl.ds(pl.multiple_of(b * PACKED_ROWS, PACKED_ROWS), PACKED_ROWS)
                for b in blocks]
        inv = []
        for k in range(NORM_GROUP):
            x = load_block(blocks[k])
            if xt_ref is not None:
                xt_ref[rows[k], :] = x
            inv.append(_inv_rms(x))
        for k in range(NORM_GROUP):
            x = load_block(blocks[k]) if xt_ref is None else xt_ref[rows[k], :]
            h_ref[rows[k], :] = (x * inv[k] * g * scale1 + shift).astype(h_ref.dtype)
        return carry

    lax.fori_loop(0, tm // (NORM_GROUP * PACKED_ROWS), body, 0)


def _zero_rows(ref, *, tm):
    def body(r, carry):
        rows = pl.ds(pl.multiple_of(r * SUBLANES, SUBLANES), SUBLANES)
        ref[rows, :] = jnp.zeros((SUBLANES, ref.shape[1]), ref.dtype)
        return carry

    lax.fori_loop(0, tm // SUBLANES, body, 0, unroll=8)


def _causal_taps(ext_ref, cw_ref, *, tm, taps):
    acc = cw_ref[0:1, :] * ext_ref[0:tm, :]
    for k in range(1, taps):
        acc = acc + cw_ref[k:k + 1, :] * ext_ref[k * SUBLANES:k * SUBLANES + tm, :]
    return acc


def _sconv_kernel(x_ref, mod_ref, g_ref, wb_ref, wc_ref, wv_ref, wg_ref, cw_ref, wo_ref,
                  o_ref, h_ref, cv_ref, carry_ref, *, tm, te, d, taps):
    i = pl.program_id(0)
    j = pl.program_id(1)
    halo = (taps - 1) * SUBLANES

    @pl.when(j == 0)
    def _():
        @pl.when(i == 0)
        def _():
            carry_ref[...] = jnp.zeros(carry_ref.shape, F32)

        def load_block(k16):
            t0 = k16 * 2
            return jnp.concatenate([x_ref[:, t0, :], x_ref[:, t0 + 1, :]], axis=0)

        _modulate_rows(load_block, o_ref, mod_ref, g_ref, h_ref, tm=tm, d=d)

    h = h_ref[...]
    cv_ref[0:halo, :] = carry_ref[j]
    cv_ref[halo:halo + tm, :] = _dot(h, wc_ref[...]) * _dot(h, wv_ref[...])
    carry_ref[j] = cv_ref[tm:tm + halo, :]
    u = _causal_taps(cv_ref, cw_ref, tm=tm, taps=taps)
    y = (_dot(h, wb_ref[...]) * u * _silu(_dot(h, wg_ref[...]))).astype(BF16)
    gate = mod_ref[:, 2 * d:3 * d]
    upd = _dot(y, wo_ref[...].astype(BF16)).reshape(tm // SUBLANES, SUBLANES, d) * gate[None]
    o_ref[...] += upd.reshape(tm, d)


def _sconv_layer(x, mod, g, w_in, conv_w, w_out, *, tm, te):
    bsz, seq, d = x.shape
    e = w_out.shape[0]
    taps = conv_w.shape[0]
    nj = e // te
    ts = tm // bsz
    halo = (taps - 1) * SUBLANES
    col = lambda c: pl.BlockSpec((d, te), lambda i, j: (0, c * nj + j))
    return pl.pallas_call(
        functools.partial(_sconv_kernel, tm=tm, te=te, d=d, taps=taps),
        grid=(seq // ts, nj),
        in_specs=[
            pl.BlockSpec((bsz, ts, d), lambda i, j: (0, i, 0)),
            pl.BlockSpec(mod.shape, lambda i, j: (0, 0)),
            pl.BlockSpec((1, d), lambda i, j: (0, 0)),
            col(0),
            col(1),
            col(2),
            col(3),
            pl.BlockSpec((taps, te), lambda i, j: (0, j)),
            pl.BlockSpec((te, d), lambda i, j: (j, 0)),
        ],
        out_specs=pl.BlockSpec((tm, d), lambda i, j: (i, 0)),
        out_shape=jax.ShapeDtypeStruct((seq * bsz, d), F32),
        scratch_shapes=[
            pltpu.VMEM((tm, d), BF16),
            pltpu.VMEM((tm + halo, te), F32),
            pltpu.VMEM((nj, halo, te), F32),
        ],
        compiler_params=pltpu.CompilerParams(
            dimension_semantics=("arbitrary", "arbitrary"),
            vmem_limit_bytes=V7X_VMEM_LIMIT_BYTES),
        name="sconv_layer",
    )(x, mod, g, w_in, w_in, w_in, w_in, conv_w, w_out)


def _rglru_kernel(x_ref, mod_ref, g_ref, wv_ref, wg_ref, p_ref, wa_ref, wx_ref, wo_ref, fg_ref,
                  o_ref, acc_ref, h_ref, v_ref, a_ref, b_ref, vcarry_ref, hcarry_ref,
                  *, tm, te, d, taps, nj):
    i = pl.program_id(0)
    j = pl.program_id(1)
    halo = (taps - 1) * SUBLANES
    cb, ba, bx, lam = (p_ref[taps + k:taps + k + 1, :] for k in range(4))

    @pl.when(j == 0)
    def _():
        @pl.when(i == 0)
        def _():
            vcarry_ref[...] = jnp.zeros(vcarry_ref.shape, F32)
            hcarry_ref[...] = jnp.zeros(hcarry_ref.shape, F32)

        def load_block(k16):
            return x_ref[pl.ds(pl.multiple_of(k16 * PACKED_ROWS, PACKED_ROWS), PACKED_ROWS), :]

        _modulate_rows(load_block, None, mod_ref, g_ref, h_ref, tm=tm, d=d)
        _zero_rows(acc_ref, tm=tm)

    h = h_ref[...]
    v_ref[0:halo, :] = vcarry_ref[j]
    v_ref[halo:halo + tm, :] = _dot(h, wv_ref[...])
    vcarry_ref[j] = v_ref[tm:tm + halo, :]
    vc = cb + _causal_taps(v_ref, p_ref, tm=tm, taps=taps)

    vcb = vc.astype(BF16)
    za, zx = [], []
    for hh in range(te // LRU_HEAD_DIM):
        cols = slice(hh * LRU_HEAD_DIM, (hh + 1) * LRU_HEAD_DIM)
        za.append(_dot(vcb[:, cols], wa_ref[hh].astype(BF16)))
        zx.append(_dot(vcb[:, cols], wx_ref[hh].astype(BF16)))
    r = _sigmoid(jnp.concatenate(za, axis=-1) + ba)
    ig = _sigmoid(jnp.concatenate(zx, axis=-1) + bx)

    z = -lam
    softplus = jnp.maximum(z, 0.0) + jnp.log1p(jnp.exp(-jnp.abs(z)))
    log_a = r * (-RGLRU_C * softplus)
    a = jnp.exp(log_a)
    norm = _sqrt_nonneg(-jnp.tanh(log_a) * (a * a + 1.0))
    a_ref[...] = a
    b_ref[...] = norm * (ig * vc)

    hprev = hcarry_ref[j]
    for t in range(tm // SUBLANES):
        rows = slice(t * SUBLANES, (t + 1) * SUBLANES)
        hprev = a_ref[rows, :] * hprev + b_ref[rows, :]
        b_ref[rows, :] = hprev
    hcarry_ref[j] = hprev

    pg = _dot(h, wg_ref[...])
    y = (b_ref[...] * _silu(pg)).astype(BF16)
    acc_ref[...] += _dot(y, wo_ref[...].astype(BF16))

    @pl.when(j == nj - 1)
    def _():
        gate = mod_ref[:, 2 * d:3 * d]
        fg = fg_ref[...]
        group = 2 * NORM_GROUP

        def body(r_, carry):
            steps = [r_ * group + k for k in range(group)]
            rows = [pl.ds(pl.multiple_of(t * SUBLANES, SUBLANES), SUBLANES) for t in steps]
            inv = []
            for k in range(group):
                xn = x_ref[rows[k], :] + gate * acc_ref[rows[k], :]
                acc_ref[rows[k], :] = xn
                inv.append(_inv_rms(xn))
            for k in range(group):
                o_ref[:, steps[k], :] = acc_ref[rows[k], :] * inv[k] * fg
            return carry

        lax.fori_loop(0, tm // (group * SUBLANES), body, 0)


def _rglru_layer(xt, mod, g, w_in, conv_w, conv_b, w_a, b_a, w_x, b_x, lam, w_out,
                 final_g, *, bsz, tm, te):
    t, d = xt.shape
    e = w_out.shape[0]
    taps = conv_w.shape[0]
    nj = e // te
    ts = tm // bsz
    hps = te // LRU_HEAD_DIM
    halo = (taps - 1) * SUBLANES
    row = lambda a: a.reshape(1, e)
    params = jnp.concatenate([conv_w, row(conv_b), row(b_a), row(b_x), row(lam)], axis=0)
    cols = lambda i, j: (0, j)
    heads = lambda i, j: (j, 0, 0)
    return pl.pallas_call(
        functools.partial(_rglru_kernel, tm=tm, te=te, d=d, taps=taps, nj=nj),
        grid=(t // tm, nj),
        in_specs=[
            pl.BlockSpec((tm, d), lambda i, j: (i, 0)),
            pl.BlockSpec(mod.shape, lambda i, j: (0, 0)),
            pl.BlockSpec((1, d), lambda i, j: (0, 0)),
            pl.BlockSpec((d, te), cols),
            pl.BlockSpec((d, te), lambda i, j: (0, nj + j)),
            pl.BlockSpec((taps + 4, te), cols),
            pl.BlockSpec((hps, LRU_HEAD_DIM, LRU_HEAD_DIM), heads),
            pl.BlockSpec((hps, LRU_HEAD_DIM, LRU_HEAD_DIM), heads),
            pl.BlockSpec((te, d), lambda i, j: (j, 0)),
            pl.BlockSpec((1, d), lambda i, j: (0, 0)),
        ],
        out_specs=pl.BlockSpec((bsz, ts, d), lambda i, j: (0, i, 0)),
        out_shape=jax.ShapeDtypeStruct((bsz, t // bsz, d), F32),
        scratch_shapes=[
            pltpu.VMEM((tm, d), F32),
            pltpu.VMEM((tm, d), BF16),
            pltpu.VMEM((tm + halo, te), F32),
            pltpu.VMEM((tm, te), F32),
            pltpu.VMEM((tm, te), F32),
            pltpu.VMEM((nj, halo, te), F32),
            pltpu.VMEM((nj, SUBLANES, te), F32),
        ],
        compiler_params=pltpu.CompilerParams(
            dimension_semantics=("arbitrary", "arbitrary"),
            vmem_limit_bytes=V7X_VMEM_LIMIT_BYTES),
        name="rglru_layer",
    )(xt, mod, g, w_in, w_in, params, w_a, w_x, w_out, final_g.reshape(1, d))


def kernel(x, c, norm_g, ada_w, ada_b, sc_w_in, sc_conv_w, sc_w_out, lru_w_in, lru_conv_w,
           lru_conv_b, lru_w_a, lru_b_a, lru_w_x, lru_b_x, lru_lambda, lru_w_out, final_g):
    bsz, seq, d = x.shape
    assert bsz == SUBLANES, "rows are grouped so that one f32 vreg holds all sequences"
    assert ada_w.shape[0] == 2 and sc_w_in.shape[0] == 1 and lru_w_in.shape[0] == 1

    mod = _adaln_mod(c, ada_w, ada_b, tn=ADALN_COLS)
    tm, te = SCONV_TILE
    x1 = _sconv_layer(x, mod[0], norm_g[0:1], sc_w_in[0].astype(BF16), sc_conv_w[0],
                      sc_w_out[0], tm=tm, te=te)
    tm, te = RGLRU_TILE
    return _rglru_layer(x1, mod[1], norm_g[1:2], lru_w_in[0].astype(BF16), lru_conv_w[0],
                        lru_conv_b[0], lru_w_a[0], lru_b_a[0], lru_w_x[0], lru_b_x[0],
                        lru_lambda[0], lru_w_out[0], final_g, bsz=bsz, tm=tm, te=te)
```
